```python
import math
import jax, jax.numpy as jnp
from jax import lax
import numpy as np

D_MODEL = 1024
BATCH = 32
SEQ = 2048
DEPTH = 1

CHUNK = 64
Q_BLOCK = 128
EPS = 1e-6
H_A = 4
DH_A = 64
W_A = H_A * 2 * DH_A
H_B = 8
DH_B = 64
W_B = H_B * DH_B
H_I = 4
D_I = 64
TOPK_MAX = 256
N_MEM = 256
H_X = 4
DH_X = D_MODEL // H_X
N_GROUPS = 4
EXP_PER_GROUP = 8
N_EXPERTS = N_GROUPS * EXP_PER_GROUP
TOP_K_INNER = 2
D_EXPERT = 512
MOE_BLOCK = 256

IN_WIDTHS = (H_A * 2 * DH_A, H_A * 2 * DH_A, H_A * 2 * DH_A, H_B * DH_B, DH_B, DH_B, H_I * D_I, D_I, H_I)
D_IN = sum(IN_WIDTHS)
IN_SPLITS = tuple(sum(IN_WIDTHS[:i + 1]) for i in range(len(IN_WIDTHS) - 1))

kernel_name = "hybrid_diff_dsa_hiermoe_block"


def rms_norm(x, g):
    xf = x.astype(jnp.float32)
    y = xf * lax.rsqrt(jnp.mean(xf * xf, axis=-1, keepdims=True) + EPS)
    return (y * g.astype(jnp.float32)).astype(x.dtype)


def alibi_slopes(n):
    return jnp.asarray([2.0 ** (-8.0 * (i + 1) / n) for i in range(n)], dtype=jnp.float32)


def to_blocks(a):
    b, s = a.shape[:2]
    a = a.reshape((b, s // Q_BLOCK, Q_BLOCK) + a.shape[2:])
    return jnp.moveaxis(a, 1, 0)


def from_blocks(o):
    o = jnp.moveaxis(o, 0, 1)
    return o.reshape((o.shape[0], o.shape[1] * o.shape[2]) + o.shape[3:])


def chunk_allowed(t, s):
    return (t[:, None] // CHUNK) >= (s[None, :] // CHUNK)


def diff_attention(q, k, v, lam, slopes):
    s_len = q.shape[1]
    pos = jnp.arange(s_len, dtype=jnp.int32)
    scale = DH_A ** -0.5

    def one_block(args):
        q_blk, t = args
        sc = jnp.einsum('bqhmd,bshmd->bhmqs', q_blk, k).astype(jnp.float32) * scale
        dist = jnp.abs(t[:, None] - pos[None, :]).astype(jnp.float32)
        sc = sc - slopes[None, :, None, None, None] * dist
        sc = jnp.where(chunk_allowed(t, pos), sc, -jnp.inf)
        p = jax.nn.softmax(sc, axis=-1)
        a = p[:, :, 0] - lam * p[:, :, 1]
        return jnp.einsum('bhqs,bshe->bqhe', a.astype(v.dtype), v)

    o = lax.map(one_block, (to_blocks(q), pos.reshape(-1, Q_BLOCK)))
    return from_blocks(o)


def dsa_attention(q, k, v, q_idx, k_idx, w_idx, slopes, k_sel):
    s_len = q.shape[1]
    pos = jnp.arange(s_len, dtype=jnp.int32)
    scale = DH_B ** -0.5
    gather = jax.vmap(lambda arr, ii: arr[ii])

    def one_block(args):
        q_blk, qi_blk, wi_blk, t = args
        dots = jnp.einsum('bqhd,bsd->bqhs', qi_blk, k_idx).astype(jnp.float32)
        isc = jnp.einsum('bqhs,bqh->bqs', jax.nn.relu(dots), wi_blk.astype(jnp.float32))
        isc = jnp.where(chunk_allowed(t, pos)[None], isc, -jnp.inf)
        top_val, top_idx = lax.top_k(isc, k_sel)
        valid = top_val > -jnp.inf
        k_g = gather(k, top_idx)
        v_g = gather(v, top_idx)
        sc = jnp.einsum('bqhd,bqkd->bhqk', q_blk, k_g).astype(jnp.float32) * scale
        dist = jnp.abs(t[None, :, None] - top_idx).astype(jnp.float32)
        sc = sc - slopes[None, :, None, None] * dist[:, None]
        sc = jnp.where(valid[:, None], sc, -jnp.inf)
        p = jax.nn.softmax(sc, axis=-1)
        return jnp.einsum('bhqk,bqkd->bqhd', p.astype(v.dtype), v_g)

    o = lax.map(one_block, (to_blocks(q), to_blocks(q_idx), to_blocks(w_idx), pos.reshape(-1, Q_BLOCK)))
    return from_blocks(o)


def memory_cross_attention(xn, memn, w_cq, w_ckv, w_co):
    b, s, d = xn.shape
    m = memn.shape[1]
    q = (xn @ w_cq).reshape(b, s, H_X, DH_X)
    k, v = jnp.split((memn @ w_ckv).reshape(m and b, m, 2, H_X, DH_X), 2, axis=2)
    k, v = k[:, :, 0], v[:, :, 0]
    sc = jnp.einsum('bshd,bmhd->bhsm', q, k).astype(jnp.float32) * (DH_X ** -0.5)
    p = jax.nn.softmax(sc, axis=-1)
    o = jnp.einsum('bhsm,bmhd->bshd', p.astype(v.dtype), v).reshape(b, s, d)
    return o @ w_co


def hier_moe(x2, w_group, b_group, w_router, b_router, w1, w3, w2):
    n, d = x2.shape
    g_logits = (x2 @ w_group).astype(jnp.float32) + b_group.astype(jnp.float32)
    p_group = jax.nn.softmax(g_logits, axis=-1)
    g_idx = jnp.argmax(g_logits, axis=-1).astype(jnp.int32)
    p_g = jnp.take_along_axis(p_group, g_idx[:, None], axis=1)[:, 0]
    e_logits = ((x2 @ w_router).astype(jnp.float32) + b_router.astype(jnp.float32)).reshape(n, N_GROUPS, EXP_PER_GROUP)
    e_in = jnp.take_along_axis(e_logits, g_idx[:, None, None], axis=1)[:, 0]
    top_p, top_i = lax.top_k(jax.nn.softmax(e_in, axis=-1), TOP_K_INNER)
    gate = p_g[:, None] * top_p / jnp.sum(top_p, axis=-1, keepdims=True)
    expert = g_idx[:, None] * EXP_PER_GROUP + top_i.astype(jnp.int32)

    flat_e = expert.reshape(-1)
    n_assign = flat_e.shape[0]
    flat_tok = jnp.repeat(jnp.arange(n, dtype=jnp.int32), TOP_K_INNER)
    flat_gate = gate.reshape(-1)
    order = jnp.argsort(flat_e)
    sorted_e = flat_e[order]
    counts = jnp.bincount(flat_e, length=N_EXPERTS).astype(jnp.int32)
    padded = (counts + MOE_BLOCK - 1) // MOE_BLOCK * MOE_BLOCK
    pad_end = jnp.cumsum(padded)
    pad_start = pad_end - padded
    start = jnp.cumsum(counts) - counts
    rank = jnp.arange(n_assign, dtype=jnp.int32) - start[sorted_e]
    dest = pad_start[sorted_e] + rank
    n_blocks = -(-n_assign // MOE_BLOCK) + N_EXPERTS
    n_slots = n_blocks * MOE_BLOCK
    slot_tok = jnp.full((n_slots,), n, dtype=jnp.int32).at[dest].set(flat_tok[order])
    slot_gate = jnp.zeros((n_slots,), jnp.float32).at[dest].set(flat_gate[order])
    block_expert = jnp.minimum(
        jnp.searchsorted(pad_end, jnp.arange(n_blocks, dtype=jnp.int32) * MOE_BLOCK, side='right'),
        N_EXPERTS - 1).astype(jnp.int32)
    x_pad = jnp.concatenate([x2, jnp.zeros((1, d), x2.dtype)], axis=0)

    def expert_block(args):
        tok, gte, e = args
        xb = x_pad[tok]
        hb = jax.nn.silu(xb @ w1[e]) * (xb @ w3[e])
        return (hb @ w2[e]) * gte[:, None].astype(x2.dtype)

    y_slots = lax.map(expert_block, (slot_tok.reshape(n_blocks, MOE_BLOCK),
                                     slot_gate.reshape(n_blocks, MOE_BLOCK), block_expert))
    y = jnp.zeros((n + 1, d), x2.dtype).at[slot_tok].add(y_slots.reshape(n_slots, d))
    return y[:n]


def setup_inputs(seed: int = 0) -> dict:
    key = jax.random.key(seed)
    ks = jax.random.split(key, 32)

    def nrm(k, shape, scale):
        return jax.random.normal(k, shape, jnp.float32) * scale

    def gain(k, shape):
        return 1.0 + 0.02 * jax.random.normal(k, shape, jnp.float32)

    L, D = DEPTH, D_MODEL
    return {
        "x": nrm(ks[0], (BATCH, SEQ, D), 1.0),
        "mem": nrm(ks[1], (BATCH, N_MEM, D), 1.0),
        "mix_norm": gain(ks[2], (L, D)),
        "w_in": nrm(ks[3], (L, D, D_IN), D ** -0.5),
        "lam_q1": nrm(ks[4], (L, DH_A), 0.1),
        "lam_k1": nrm(ks[5], (L, DH_A), 0.1),
        "lam_q2": nrm(ks[6], (L, DH_A), 0.1),
        "lam_k2": nrm(ks[7], (L, DH_A), 0.1),
        "diff_subln": gain(ks[8], (L, 2 * DH_A)),
        "w_branch_a": nrm(ks[9], (L, W_A, D), W_A ** -0.5),
        "w_branch_b": nrm(ks[10], (L, W_B, D), W_B ** -0.5),
        "w_gate": nrm(ks[11], (L, D, 2 * D), D ** -0.5),
        "b_gate": nrm(ks[12], (L, 2 * D), 0.02),
        "w_out": nrm(ks[13], (L, D, D), D ** -0.5),
        "cross_norm": gain(ks[14], (L, D)),
        "mem_norm": gain(ks[15], (L, D)),
        "w_cq": nrm(ks[16], (L, D, D), D ** -0.5),
        "w_ckv": nrm(ks[17], (L, D, 2 * D), D ** -0.5),
        "w_co": nrm(ks[18], (L, D, D), D ** -0.5),
        "ffn_norm": gain(ks[19], (L, D)),
        "w_group": nrm(ks[20], (L, D, N_GROUPS), D ** -0.5),
        "b_group": nrm(ks[21], (L, N_GROUPS), 0.01),
        "w_router": nrm(ks[22], (L, D, N_EXPERTS), D ** -0.5),
        "b_router": nrm(ks[23], (L, N_EXPERTS), 0.01),
        "w1": nrm(ks[24], (L, N_EXPERTS, D, D_EXPERT), D ** -0.5),
        "w3": nrm(ks[25], (L, N_EXPERTS, D, D_EXPERT), D ** -0.5),
        "w2": nrm(ks[26], (L, N_EXPERTS, D_EXPERT, D), D_EXPERT ** -0.5),
        "final_norm": gain(ks[27], (D,)),
    }


def reference(x, mem, mix_norm, w_in, lam_q1, lam_k1, lam_q2, lam_k2, diff_subln, w_branch_a, w_branch_b,
              w_gate, b_gate, w_out, cross_norm, mem_norm, w_cq, w_ckv, w_co, ffn_norm, w_group, b_group,
              w_router, b_router, w1, w3, w2, final_norm):
    b, s, d = x.shape
    k_sel = min(TOPK_MAX, s // 4)
    slopes_a = alibi_slopes(H_A)
    slopes_b = alibi_slopes(H_B)
    h = x
    for l in range(DEPTH):
        lambda_init = 0.8 - 0.6 * math.exp(-0.3 * l)
        xn = rms_norm(h, mix_norm[l])
        q_a, k_a, v_a, q_b, k_b, v_b, q_i, k_i, w_i = jnp.split(xn @ w_in[l], IN_SPLITS, axis=-1)
        lam = (jnp.exp(jnp.sum(lam_q1[l].astype(jnp.float32) * lam_k1[l].astype(jnp.float32)))
               - jnp.exp(jnp.sum(lam_q2[l].astype(jnp.float32) * lam_k2[l].astype(jnp.float32)))
               + lambda_init)
        y_a = diff_attention(q_a.reshape(b, s, H_A, 2, DH_A), k_a.reshape(b, s, H_A, 2, DH_A),
                             v_a.reshape(b, s, H_A, 2 * DH_A), lam, slopes_a)
        y_a = (rms_norm(y_a, diff_subln[l]) * (1.0 - lambda_init)).reshape(b, s, W_A)
        y_b = dsa_attention(q_b.reshape(b, s, H_B, DH_B), k_b, v_b,
                            q_i.reshape(b, s, H_I, D_I), k_i,
                            w_i * (H_I ** -0.5 * D_I ** -0.5), slopes_b, k_sel).reshape(b, s, W_B)
        gates = jax.nn.sigmoid((xn @ w_gate[l]).astype(jnp.float32) + b_gate[l].astype(jnp.float32)).astype(h.dtype)
        g_a, g_b = jnp.split(gates, 2, axis=-1)
        merged = g_a * (y_a @ w_branch_a[l]) + g_b * (y_b @ w_branch_b[l])
        h = h + merged @ w_out[l]
        h = h + memory_cross_attention(rms_norm(h, cross_norm[l]), rms_norm(mem, mem_norm[l]),
                                       w_cq[l], w_ckv[l], w_co[l])
        y_f = hier_moe(rms_norm(h, ffn_norm[l]).reshape(b * s, d), w_group[l], b_group[l],
                       w_router[l], b_router[l], w1[l], w3[l], w2[l])
        h = h + y_f.reshape(b, s, d)
    return rms_norm(h, final_norm)
```

```python
import functools

import jax
import jax.numpy as jnp
from jax import lax
from jax.experimental import pallas as pl
from jax.experimental.pallas import tpu as pltpu

F32 = jnp.float32
BF16 = jnp.bfloat16
I32 = jnp.int32

EPS = 1e-6
CHUNK_SHIFT = 6
H_A, DH_A = 4, 64
H_B, DH_B = 8, 64
H_I, D_I = 4, 64
TOPK_MAX = 256
H_X = 4
N_GROUPS, EXP_PER_GROUP = 4, 8
N_EXPERTS = N_GROUPS * EXP_PER_GROUP
LAMBDA_INIT = 0.8 - 0.6 * 1.0
LANES = 128
NEG = -1e30
INT_MIN = -2147483648
KEY_NEG_INF = -2139095041
VMEM_LIMIT = 56 * 1024 * 1024

PROJ_WIDTHS = (512, 512, 512, 512, 128, 128, 256, 128, 128)
IN_WIDTHS = (512, 512, 512, 512, 64, 64, 256, 64, 4)


def _rms(x, g):
    ms = jnp.mean(x * x, axis=-1, keepdims=True)
    return (x * lax.rsqrt(ms + EPS)) * g


def _dot_nt(a, b):
    return lax.dot_general(a, b, (((1,), (1,)), ((), ())), preferred_element_type=F32)


def _dot(a, b):
    return jnp.dot(a, b, preferred_element_type=F32)


def _wide(x, n):
    return x if n == 1 else jnp.concatenate([x] * n, axis=1)


def _proj_kernel(x_ref, g_ref, w_ref, *out_refs):
    xn = _rms(x_ref[...], g_ref[...]).astype(BF16)
    col = 0
    for ref in out_refs:
        n = ref.shape[-1]
        ref[...] = _dot(xn, w_ref[:, col:col + n]).astype(ref.dtype)
        col += n


def _proj(x2, gain, w_all, tm):
    n, d = x2.shape
    dts = (BF16,) * 8 + (F32,)
    return pl.pallas_call(
        _proj_kernel,
        grid=(n // tm,),
        in_specs=[pl.BlockSpec((tm, d), lambda i: (i, 0)),
                  pl.BlockSpec((1, d), lambda i: (0, 0)),
                  pl.BlockSpec(w_all.shape, lambda i: (0, 0))],
        out_specs=[pl.BlockSpec((tm, w), lambda i: (i, 0)) for w in PROJ_WIDTHS],
        out_shape=[jax.ShapeDtypeStruct((n, w), dt) for w, dt in zip(PROJ_WIDTHS, dts)],
        compiler_params=pltpu.CompilerParams(dimension_semantics=("arbitrary",),
                                             vmem_limit_bytes=VMEM_LIMIT),
        name="proj",
    )(x2, gain, w_all)


def _diff_kernel(slope_ref, lam_ref, q_ref, k_ref, v_ref, g_ref, o_ref, m_sc, l_sc, acc_sc, *, tq, tk):
    h = pl.program_id(1)
    qb = pl.program_id(2)
    slope = slope_ref[h]
    lam = lam_ref[0]
    q = q_ref[...]
    lane_k = lax.broadcasted_iota(I32, (tk, LANES), 1)
    row_t = qb * tq + lax.broadcasted_iota(I32, (tq, tk), 0)
    col_j = lax.broadcasted_iota(I32, (tq, tk), 1)
    t_chunk = row_t >> CHUNK_SHIFT
    nrep = tk // LANES

    m_sc[...] = jnp.full(m_sc.shape, NEG, F32)
    l_sc[...] = jnp.zeros(l_sc.shape, F32)
    acc_sc[...] = jnp.zeros(acc_sc.shape, F32)

    def block(kb, carry):
        off = pl.multiple_of(kb * tk, tk)
        k = k_ref[pl.ds(off, tk), :]
        v = v_ref[pl.ds(off, tk), :]
        zero = jnp.zeros_like(k)
        k_m = (jnp.where(lane_k < DH_A, k, zero), jnp.where(lane_k >= DH_A, k, zero))
        s_pos = off + col_j
        bias = slope * jnp.abs(row_t - s_pos).astype(F32)
        allowed = (s_pos >> CHUNK_SHIFT) <= t_chunk
        for m in range(2):
            s = jnp.where(allowed, _dot_nt(q, k_m[m]) - bias, NEG)
            m_prev = m_sc[m]
            m_next = jnp.maximum(m_prev, jnp.max(s, axis=1, keepdims=True))
            p = jnp.exp(s - _wide(m_next, nrep))
            alpha = jnp.exp(m_prev - m_next)
            l_sc[m] = alpha * l_sc[m] + jnp.sum(p, axis=1, keepdims=True)
            acc_sc[m] = alpha * acc_sc[m] + _dot(p.astype(BF16), v)
            m_sc[m] = m_next
        return carry

    lax.fori_loop(0, (qb * tq) // tk + tq // tk, block, 0)

    y = acc_sc[0] / l_sc[0] - lam * (acc_sc[1] / l_sc[1])
    o_ref[...] = (_rms(y, g_ref[...]) * (1.0 - LAMBDA_INIT)).astype(o_ref.dtype)


def _diff_attention(qa, ka, va, slopes, lam, subln, tq, tk):
    b, s, _ = qa.shape
    kern = functools.partial(_diff_kernel, tq=tq, tk=tk)
    smem = pl.BlockSpec(memory_space=pltpu.SMEM)
    return pl.pallas_call(
        kern,
        grid=(b, H_A, s // tq),
        in_specs=[smem, smem,
                  pl.BlockSpec((None, tq, LANES), lambda bi, h, i: (bi, i, h)),
                  pl.BlockSpec((None, s, LANES), lambda bi, h, i: (bi, 0, h)),
                  pl.BlockSpec((None, s, LANES), lambda bi, h, i: (bi, 0, h)),
                  pl.BlockSpec((1, LANES), lambda bi, h, i: (0, 0))],
        out_specs=pl.BlockSpec((None, tq, LANES), lambda bi, h, i: (bi, i, h)),
        out_shape=jax.ShapeDtypeStruct((b, s, H_A * 2 * DH_A), BF16),
        scratch_shapes=[pltpu.VMEM((2, tq, LANES), F32),
                        pltpu.VMEM((2, tq, LANES), F32),
                        pltpu.VMEM((2, tq, LANES), F32)],
        compiler_params=pltpu.CompilerParams(dimension_semantics=("arbitrary",) * 3,
                                             vmem_limit_bytes=VMEM_LIMIT),
        name="diff_attn",
    )(slopes, lam, qa, ka, va, subln)


def _dsa_kernel(slope_ref, qb_ref, kv1_ref, kv2_ref, qi_ref, kik_ref, wi_ref, o_ref,
                key_sc, m_sc, acc_sc, wrep_sc, p_sc, *, tq, tk, k_sel, idx_bits):
    qblk = pl.program_id(1)
    nvis = (qblk * tq) // tk + tq // tk
    nrep = tk // LANES
    lane_k = lax.broadcasted_iota(I32, (tk, LANES), 1)
    lane_q = lax.broadcasted_iota(I32, (tq, LANES), 1)
    row_t = qblk * tq + lax.broadcasted_iota(I32, (tq, tk), 0)
    col_j = lax.broadcasted_iota(I32, (tq, tk), 1)
    t_chunk = row_t >> CHUNK_SHIFT

    wi = wi_ref[...]
    for h in range(H_I):
        wrep_sc[h] = jnp.broadcast_to(wi[:, h:h + 1], (tq, LANES))
    qidx = qi_ref[...]

    def index_block(kb, carry):
        off = pl.multiple_of(kb * tk, tk)
        kik = kik_ref[pl.ds(off, tk), :]
        zero = jnp.zeros_like(kik)
        k_half = (jnp.where(lane_k < D_I, kik, zero), jnp.where(lane_k >= D_I, kik, zero))
        isc = jnp.zeros((tq, tk), F32)
        for h in range(H_I):
            qp = qidx[:, (h // 2) * LANES:(h // 2 + 1) * LANES]
            d = _dot_nt(qp, k_half[h % 2])
            isc = isc + _wide(wrep_sc[h], nrep) * jnp.maximum(d, 0.0)
        bits = lax.bitcast_convert_type(isc, I32)
        key = jnp.where(bits < 0, bits ^ 0x7FFFFFFF, bits)
        key = jnp.where(isc == 0.0, 0, key)
        allowed = ((off + col_j) >> CHUNK_SHIFT) <= t_chunk
        key_sc[kb] = jnp.where(allowed, key, KEY_NEG_INF)
        return carry

    lax.fori_loop(0, nvis, index_block, 0)

    def count_ge(cand):
        def body(kb, acc):
            kblk = key_sc[kb]
            for c in range(nrep):
                acc = acc + jnp.where(kblk[:, c * LANES:(c + 1) * LANES] >= cand, 1.0, 0.0)
            return acc
        acc = lax.fori_loop(0, nvis, body, jnp.zeros((tq, LANES), F32))
        return jnp.sum(acc, axis=1, keepdims=True)

    kf = float(k_sel)
    zero_i = jnp.zeros((tq, LANES), I32)
    thr = jnp.where(count_ge(zero_i) >= kf, zero_i, jnp.full((tq, LANES), INT_MIN, I32))

    def bit_step(i, t):
        cand = t + lax.shift_left(jnp.int32(1), jnp.asarray(30 - i, I32))
        return jnp.where(count_ge(cand) >= kf, cand, t)

    thr = lax.fori_loop(0, 31, bit_step, thr)

    need = kf - count_ge(thr + 1)
    surplus = jnp.where(thr == KEY_NEG_INF, 0.0, count_ge(thr) - kf)
    thr_eq = jnp.where(thr == KEY_NEG_INF, INT_MIN, thr)
    p_sc[...] = jnp.full((tq, LANES), (1 << idx_bits) - 1, I32)

    @pl.when(jnp.max(surplus) > 0.0)
    def _():
        def count_eq_before(cand):
            def body(kb, acc):
                kblk = key_sc[kb]
                for c in range(nrep):
                    idx = kb * tk + c * LANES + lane_q
                    hit = jnp.where(idx < cand, 1.0, 0.0)
                    acc = acc + jnp.where(kblk[:, c * LANES:(c + 1) * LANES] == thr_eq, hit, 0.0)
                return acc
            acc = lax.fori_loop(0, nvis, body, jnp.zeros((tq, LANES), F32))
            return jnp.sum(acc, axis=1, keepdims=True)

        def idx_step(i, p):
            cand = p + lax.shift_left(jnp.int32(1), jnp.asarray(idx_bits - 1 - i, I32))
            return jnp.where(count_eq_before(cand) < need, cand, p)

        p_sc[...] = lax.fori_loop(0, idx_bits, idx_step, zero_i)

    last_eq = p_sc[...]

    m_sc[...] = jnp.full(m_sc.shape, NEG, F32)
    acc_sc[...] = jnp.zeros(acc_sc.shape, F32)
    qb = qb_ref[...]

    def attend_block(kb, carry):
        off = pl.multiple_of(kb * tk, tk)
        kv1 = kv1_ref[pl.ds(off, tk), :]
        kv2 = kv2_ref[pl.ds(off, tk), :]
        zero = jnp.zeros_like(kv1)
        one = jnp.ones_like(kv1)
        k_half = (jnp.where(lane_k < DH_B, kv1, zero), jnp.where(lane_k >= DH_B, kv2, zero))
        v_half = (jnp.where(lane_k < DH_B, kv2, one), jnp.where(lane_k >= DH_B, kv1, one))
        key = key_sc[kb]
        s_pos = off + col_j
        keep_tie = jnp.where(s_pos <= _wide(last_eq, nrep), 0.0, NEG)
        mask = jnp.where(key > _wide(thr, nrep), 0.0,
                         jnp.where(key == _wide(thr_eq, nrep), keep_tie, NEG))
        dist = jnp.abs(row_t - s_pos).astype(F32)
        for h in range(H_B):
            qp = qb[:, (h // 2) * LANES:(h // 2 + 1) * LANES]
            s = _dot_nt(qp, k_half[h % 2]) - slope_ref[h] * dist + mask
            m_prev = m_sc[h]
            m_next = jnp.maximum(m_prev, jnp.max(s, axis=1, keepdims=True))
            p = jnp.exp(s - _wide(m_next, nrep))
            alpha = jnp.exp(m_prev - m_next)
            acc_sc[h] = alpha * acc_sc[h] + _dot(p.astype(BF16), v_half[h % 2])
            m_sc[h] = m_next
        return carry

    lax.fori_loop(0, nvis, attend_block, 0)

    for pr in range(H_B // 2):
        ae = acc_sc[2 * pr]
        ao = acc_sc[2 * pr + 1]
        num = jnp.where(lane_q < DH_B, ae, ao)
        den = jnp.where(lane_q < DH_B, pltpu.roll(ae, DH_B, 1), pltpu.roll(ao, DH_B, 1))
        o_ref[:, pr * LANES:(pr + 1) * LANES] = (num / den).astype(o_ref.dtype)


def _dsa_attention(qb, kv1, kv2, qi, kik, wi, slopes, k_sel, tq, tk):
    b, s, _ = qb.shape
    idx_bits = max(1, (s - 1).bit_length())
    kern = functools.partial(_dsa_kernel, tq=tq, tk=tk, k_sel=k_sel, idx_bits=idx_bits)
    tile = lambda w: pl.BlockSpec((None, tq, w), lambda bi, i: (bi, i, 0))
    full = lambda w: pl.BlockSpec((None, s, w), lambda bi, i: (bi, 0, 0))
    return pl.pallas_call(
        kern,
        grid=(b, s // tq),
        in_specs=[pl.BlockSpec(memory_space=pltpu.SMEM),
                  tile(H_B * DH_B), full(LANES), full(LANES), tile(H_I * D_I), full(LANES), tile(LANES)],
        out_specs=tile(H_B * DH_B),
        out_shape=jax.ShapeDtypeStruct((b, s, H_B * DH_B), BF16),
        scratch_shapes=[pltpu.VMEM((s // tk, tq, tk), I32),
                        pltpu.VMEM((H_B, tq, LANES), F32),
                        pltpu.VMEM((H_B, tq, LANES), F32),
                        pltpu.VMEM((H_I, tq, LANES), F32),
                        pltpu.VMEM((tq, LANES), I32)],
        compiler_params=pltpu.CompilerParams(dimension_semantics=("arbitrary",) * 2,
                                             vmem_limit_bytes=VMEM_LIMIT),
        name="dsa_attn",
    )(slopes, qb, kv1, kv2, qi, kik, wi)


def _memkv_kernel(m_ref, g_ref, w_ref, o_ref):
    mn = _rms(m_ref[...], g_ref[...]).astype(BF16)
    o_ref[...] = _dot(mn, w_ref[...]).astype(o_ref.dtype)


def _memkv(mem2, gain, w_ckv, tm):
    n, d = mem2.shape
    return pl.pallas_call(
        _memkv_kernel,
        grid=(n // tm,),
        in_specs=[pl.BlockSpec((tm, d), lambda i: (i, 0)),
                  pl.BlockSpec((1, d), lambda i: (0, 0)),
                  pl.BlockSpec(w_ckv.shape, lambda i: (0, 0))],
        out_specs=pl.BlockSpec((tm, w_ckv.shape[1]), lambda i: (i, 0)),
        out_shape=jax.ShapeDtypeStruct((n, w_ckv.shape[1]), BF16),
        compiler_params=pltpu.CompilerParams(dimension_semantics=("arbitrary",),
                                             vmem_limit_bytes=VMEM_LIMIT),
        name="memkv",
    )(mem2, gain, w_ckv)


def _merge_kernel(x_ref, ya_ref, yb_ref, mkv_ref, mixg_ref, wgate_ref, bgate_ref, wa_ref, wb_ref, wout_ref,
                  crossg_ref, wcq_ref, wco_ref, ffng_ref, wrh_ref, wrl_ref, h2_ref, lg_ref):
    d = x_ref.shape[-1]
    dh = d // H_X
    x = x_ref[...]
    xn = _rms(x, mixg_ref[...]).astype(BF16)
    gates = jax.nn.sigmoid(_dot(xn, wgate_ref[...]) + bgate_ref[...])
    merged = gates[:, :d] * _dot(ya_ref[...], wa_ref[...]) + gates[:, d:] * _dot(yb_ref[...], wb_ref[...])
    h1 = x + _dot(merged.astype(BF16), wout_ref[...])

    q = _dot(_rms(h1, crossg_ref[...]).astype(BF16), wcq_ref[...]).astype(BF16)
    heads = []
    for h in range(H_X):
        k = mkv_ref[:, h * dh:(h + 1) * dh]
        v = mkv_ref[:, d + h * dh:d + (h + 1) * dh]
        s = _dot_nt(q[:, h * dh:(h + 1) * dh], k)
        e = jnp.exp(s - jnp.max(s, axis=1, keepdims=True))
        p = e / jnp.sum(e, axis=1, keepdims=True)
        heads.append(_dot(p.astype(BF16), v).astype(BF16))
    h2 = h1 + _dot(jnp.concatenate(heads, axis=1), wco_ref[...])
    h2_ref[...] = h2

    f = _rms(h2, ffng_ref[...])
    f_hi = f.astype(BF16)
    f_lo = (f - f_hi.astype(F32)).astype(BF16)
    w_hi = wrh_ref[...]
    lg_ref[...] = _dot_nt(w_hi, f_hi) + _dot_nt(w_hi, f_lo) + _dot_nt(wrl_ref[...], f_hi)


def _merge(x3, ya, yb, mkv, p, tm):
    b, s, d = x3.shape
    m = mkv.shape[1]
    tok = lambda w: pl.BlockSpec((None, tm, w), lambda bi, i: (bi, i, 0))
    const = lambda a: pl.BlockSpec(a.shape, lambda bi, i: (0,) * a.ndim)
    consts = (p["mix_g"], p["w_gate"], p["b_gate"], p["w_a"], p["w_b"], p["w_out"], p["cross_g"],
              p["w_cq"], p["w_co"], p["ffn_g"], p["w_r_hi"], p["w_r_lo"])
    return pl.pallas_call(
        _merge_kernel,
        grid=(b, s // tm),
        in_specs=[tok(d), tok(ya.shape[-1]), tok(yb.shape[-1]),
                  pl.BlockSpec((None, m, 2 * d), lambda bi, i: (bi, 0, 0))] + [const(a) for a in consts],
        out_specs=[tok(d), pl.BlockSpec((LANES, tm), lambda bi, i: (0, bi * (s // tm) + i))],
        out_shape=[jax.ShapeDtypeStruct((b, s, d), F32), jax.ShapeDtypeStruct((LANES, b * s), F32)],
        compiler_params=pltpu.CompilerParams(dimension_semantics=("arbitrary",) * 2,
                                             vmem_limit_bytes=VMEM_LIMIT),
        name="merge",
    )(x3, ya, yb, mkv, *consts)


def _route_kernel(lg_ref, bias_ref, mi_ref, mf_ref, cnt_ref, carry_sc, *, tm):
    @pl.when(pl.program_id(0) == 0)
    def _():
        carry_sc[...] = jnp.zeros(carry_sc.shape, F32)

    lg = lg_ref[...] + bias_ref[...]
    e = lg[0:N_EXPERTS]
    g = lg[N_EXPERTS:N_EXPERTS + 8]
    row_g = lax.broadcasted_iota(I32, (8, tm), 0)
    g = jnp.where(row_g < N_GROUPS, g, -jnp.inf)
    gmax = jnp.max(g, axis=0, keepdims=True)
    g_idx = jnp.min(jnp.where(g == gmax, row_g, N_GROUPS), axis=0, keepdims=True)
    p_g = 1.0 / jnp.sum(jnp.exp(g - gmax), axis=0, keepdims=True)

    row_e = lax.broadcasted_iota(I32, (N_EXPERTS, tm), 0)
    in_grp = (row_e >> 3) == g_idx
    emax = jnp.max(jnp.where(in_grp, e, -jnp.inf), axis=0, keepdims=True)
    ex = jnp.where(in_grp, jnp.exp(jnp.where(in_grp, e - emax, 0.0)), 0.0)
    probs = ex / jnp.sum(ex, axis=0, keepdims=True)
    big = N_EXPERTS
    p1 = jnp.max(probs, axis=0, keepdims=True)
    i1 = jnp.min(jnp.where(in_grp, jnp.where(probs == p1, row_e, big), big), axis=0, keepdims=True)
    probs2 = jnp.where(in_grp, jnp.where(row_e == i1, -1.0, probs), -1.0)
    p2 = jnp.max(probs2, axis=0, keepdims=True)
    i2 = jnp.min(jnp.where(probs2 == p2, row_e, big), axis=0, keepdims=True)
    denom = p1 + p2
    gate1 = p_g * p1 / denom
    gate2 = p_g * p2 / denom

    used = jnp.where(row_e == i1, 1.0, jnp.where(row_e == i2, 1.0, 0.0))
    r = lax.broadcasted_iota(I32, (tm, tm), 0)
    c = lax.broadcasted_iota(I32, (tm, tm), 1)
    before = jnp.where(r < c, 1.0, 0.0).astype(BF16)
    excl = _dot(used.astype(BF16), before) + carry_sc[:, 0:1]
    rank1 = jnp.sum(jnp.where(row_e == i1, excl, 0.0), axis=0, keepdims=True)
    rank2 = jnp.sum(jnp.where(row_e == i2, excl, 0.0), axis=0, keepdims=True)
    carry_sc[...] = carry_sc[...] + jnp.sum(used, axis=1, keepdims=True)
    cnt_ref[...] = carry_sc[...]

    mi_ref[...] = jnp.zeros(mi_ref.shape, I32)
    mi_ref[0:1, :] = i1
    mi_ref[1:2, :] = i2
    mi_ref[2:3, :] = rank1.astype(I32)
    mi_ref[3:4, :] = rank2.astype(I32)
    mf_ref[...] = jnp.zeros(mf_ref.shape, F32)
    mf_ref[0:1, :] = gate1
    mf_ref[1:2, :] = gate2


def _route(lg_t, bias, tm):
    n = lg_t.shape[1]
    kern = functools.partial(_route_kernel, tm=tm)
    return pl.pallas_call(
        kern,
        grid=(n // tm,),
        in_specs=[pl.BlockSpec((LANES, tm), lambda i: (0, i)),
                  pl.BlockSpec((LANES, 1), lambda i: (0, 0))],
        out_specs=[pl.BlockSpec((8, tm), lambda i: (0, i)),
                   pl.BlockSpec((8, tm), lambda i: (0, i)),
                   pl.BlockSpec((N_EXPERTS, LANES), lambda i: (0, 0))],
        out_shape=[jax.ShapeDtypeStruct((8, n), I32), jax.ShapeDtypeStruct((8, n), F32),
                   jax.ShapeDtypeStruct((N_EXPERTS, LANES), F32)],
        scratch_shapes=[pltpu.VMEM((N_EXPERTS, LANES), F32)],
        compiler_params=pltpu.CompilerParams(dimension_semantics=("arbitrary",),
                                             vmem_limit_bytes=VMEM_LIMIT),
        name="route",
    )(lg_t, bias)


def _dispatch_kernel(ps_ref, mi_ref, h2_hbm, xs_in_hbm, xs_hbm, sem, *, tm):
    del xs_in_hbm
    base = pl.program_id(0) * tm

    def row_copy(src_row, dst_row):
        return pltpu.make_async_copy(h2_hbm.at[pl.ds(src_row, 1)], xs_hbm.at[pl.ds(dst_row, 1)], sem)

    def issue(t, carry):
        for j in range(2):
            row_copy(base + t, ps_ref[mi_ref[j, t]] + mi_ref[2 + j, t]).start()
        return carry

    def drain(t, carry):
        for j in range(2):
            row_copy(0, 0).wait()
        return carry

    lax.fori_loop(0, tm, issue, 0)
    lax.fori_loop(0, tm, drain, 0)


def _dispatch(pad_start, mi, h2, n_slots, tm):
    n, d = h2.shape
    kern = functools.partial(_dispatch_kernel, tm=tm)
    xs0 = jnp.zeros((n_slots, d), F32)
    return pl.pallas_call(
        kern,
        grid=(n // tm,),
        in_specs=[pl.BlockSpec(memory_space=pltpu.SMEM),
                  pl.BlockSpec((8, tm), lambda i: (0, i), memory_space=pltpu.SMEM),
                  pl.BlockSpec(memory_space=pl.ANY),
                  pl.BlockSpec(memory_space=pl.ANY)],
        out_specs=pl.BlockSpec(memory_space=pl.ANY),
        out_shape=jax.ShapeDtypeStruct((n_slots, d), F32),
        scratch_shapes=[pltpu.SemaphoreType.DMA(())],
        input_output_aliases={3: 0},
        compiler_params=pltpu.CompilerParams(dimension_semantics=("arbitrary",),
                                             has_side_effects=True),
        name="dispatch",
    )(pad_start, mi, h2, xs0)


def _expert_kernel(be_ref, nu_ref, x_ref, g_ref, w1_ref, w3_ref, w2_ref, y_ref):
    del be_ref

    @pl.when(pl.program_id(0) < nu_ref[0])
    def _():
        xb = _rms(x_ref[...], g_ref[...]).astype(BF16)
        a = _dot(xb, w1_ref[...])
        hb = (a * jax.nn.sigmoid(a)) * _dot(xb, w3_ref[...])
        y_ref[...] = _dot(hb.astype(BF16), w2_ref[...])

    @pl.when(pl.program_id(0) >= nu_ref[0])
    def _():
        y_ref[...] = jnp.zeros(y_ref.shape, y_ref.dtype)


def _experts(block_expert, n_used, xs, gain, w1, w3, w2, tb):
    n_slots, d = xs.shape
    de = w1.shape[-1]
    nb = n_slots // tb
    row = lambda i, be, nu: (i, 0)
    grid_spec = pltpu.PrefetchScalarGridSpec(
        num_scalar_prefetch=2,
        grid=(nb,),
        in_specs=[pl.BlockSpec((tb, d), row),
                  pl.BlockSpec((1, d), lambda i, be, nu: (0, 0)),
                  pl.BlockSpec((None, d, de), lambda i, be, nu: (be[i], 0, 0)),
                  pl.BlockSpec((None, d, de), lambda i, be, nu: (be[i], 0, 0)),
                  pl.BlockSpec((None, de, d), lambda i, be, nu: (be[i], 0, 0))],
        out_specs=pl.BlockSpec((tb, d), row),
    )
    return pl.pallas_call(
        _expert_kernel,
        grid_spec=grid_spec,
        out_shape=jax.ShapeDtypeStruct((n_slots, d), F32),
        compiler_params=pltpu.CompilerParams(dimension_semantics=("arbitrary",),
                                             vmem_limit_bytes=VMEM_LIMIT),
        name="experts",
    )(block_expert, n_used, xs, gain, w1, w3, w2)


def _combine_kernel(ps_ref, mi_ref, mf_ref, h2_ref, g_ref, y_hbm, o_ref, ybuf, sem, *, tm):
    def row_copy(src_row, j, t):
        return pltpu.make_async_copy(y_hbm.at[pl.ds(src_row, 1)], ybuf.at[j, pl.ds(t, 1)], sem)

    def issue(t, carry):
        for j in range(2):
            row_copy(ps_ref[mi_ref[j, t]] + mi_ref[2 + j, t], j, t).start()
        return carry

    def drain(t, carry):
        for j in range(2):
            row_copy(0, j, 0).wait()
        return carry

    lax.fori_loop(0, tm, issue, 0)
    lax.fori_loop(0, tm, drain, 0)

    gates = jnp.concatenate([mf_ref[...], jnp.zeros((LANES - 8, tm), F32)], axis=0)
    gt = jnp.transpose(gates)
    h = h2_ref[...] + gt[:, 0:1] * ybuf[0] + gt[:, 1:2] * ybuf[1]
    o_ref[...] = _rms(h, g_ref[...])


def _combine(pad_start, mi, mf, h2, gain, y, tm):
    n, d = h2.shape
    kern = functools.partial(_combine_kernel, tm=tm)
    return pl.pallas_call(
        kern,
        grid=(n // tm,),
        in_specs=[pl.BlockSpec(memory_space=pltpu.SMEM),
                  pl.BlockSpec((8, tm), lambda i: (0, i), memory_space=pltpu.SMEM),
                  pl.BlockSpec((8, tm), lambda i: (0, i)),
                  pl.BlockSpec((tm, d), lambda i: (i, 0)),
                  pl.BlockSpec((1, d), lambda i: (0, 0)),
                  pl.BlockSpec(memory_space=pl.ANY)],
        out_specs=pl.BlockSpec((tm, d), lambda i: (i, 0)),
        out_shape=jax.ShapeDtypeStruct((n, d), F32),
        scratch_shapes=[pltpu.VMEM((2, tm, d), F32), pltpu.SemaphoreType.DMA(())],
        compiler_params=pltpu.CompilerParams(dimension_semantics=("arbitrary",),
                                             vmem_limit_bytes=VMEM_LIMIT),
        name="combine",
    )(pad_start, mi, mf, h2, gain, y)


def _alibi_slopes(n):
    return jnp.asarray([2.0 ** (-8.0 * (i + 1) / n) for i in range(n)], dtype=F32)


def _prep_w_in(w_in):
    splits = []
    acc = 0
    for w in IN_WIDTHS[:-1]:
        acc += w
        splits.append(acc)
    qa, ka, va, qb, kb, vb, qi, ki, wi = jnp.split(w_in, splits, axis=1)
    pad = jnp.zeros((w_in.shape[0], LANES - IN_WIDTHS[-1]), w_in.dtype)
    cols = [qa * DH_A ** -0.5, ka, va, qb * DH_B ** -0.5, kb, vb, vb, kb, qi, ki, ki,
            wi * (H_I ** -0.5 * D_I ** -0.5), pad]
    return jnp.concatenate(cols, axis=1).astype(BF16)


def _prep_router(w_group, w_router):
    d = w_group.shape[0]
    w = jnp.concatenate([w_router, w_group, jnp.zeros((d, LANES - N_EXPERTS - N_GROUPS), F32)], axis=1).T
    hi = w.astype(BF16)
    lo = (w - hi.astype(F32)).astype(BF16)
    return hi, lo


def _block_sizes(s):
    tq = min(256, s)
    return dict(tm_proj=512, tq=tq, tk=tq, tm_merge=min(256, s), tm_route=512, tm_disp=512, tb=256, tm_comb=256)


def kernel(x, mem, mix_norm, w_in, lam_q1, lam_k1, lam_q2, lam_k2, diff_subln, w_branch_a, w_branch_b, w_gate,
           b_gate, w_out, cross_norm, mem_norm, w_cq, w_ckv, w_co, ffn_norm, w_group, b_group, w_router,
           b_router, w1, w3, w2, final_norm):
    b, s, d = x.shape
    n = b * s
    m = mem.shape[1]
    bs = _block_sizes(s)
    k_sel = min(TOPK_MAX, s // 4)
    row = lambda v: v.reshape(1, -1).astype(F32)

    outs = _proj(x.reshape(n, d), row(mix_norm[0]), _prep_w_in(w_in[0]), bs["tm_proj"])
    qa, ka, va, qb, kv1, kv2, qi, kik, wi = [o.reshape(b, s, -1) for o in outs]
    lam = (jnp.exp(jnp.sum(lam_q1[0].astype(F32) * lam_k1[0].astype(F32)))
           - jnp.exp(jnp.sum(lam_q2[0].astype(F32) * lam_k2[0].astype(F32))) + LAMBDA_INIT).reshape(1)
    ya = _diff_attention(qa, ka, va, _alibi_slopes(H_A), lam, row(diff_subln[0]), bs["tq"], bs["tk"])
    yb = _dsa_attention(qb, kv1, kv2, qi, kik, wi, _alibi_slopes(H_B), k_sel, bs["tq"], bs["tk"])

    mkv = _memkv(mem.reshape(b * m, d), row(mem_norm[0]), w_ckv[0].astype(BF16), min(512, b * m))
    w_r_hi, w_r_lo = _prep_router(w_group[0], w_router[0])
    params = dict(mix_g=row(mix_norm[0]), w_gate=w_gate[0].astype(BF16), b_gate=row(b_gate[0]),
                  w_a=w_branch_a[0].astype(BF16), w_b=w_branch_b[0].astype(BF16), w_out=w_out[0].astype(BF16),
                  cross_g=row(cross_norm[0]), w_cq=(w_cq[0] * (d // H_X) ** -0.5).astype(BF16),
                  w_co=w_co[0].astype(BF16), ffn_g=row(ffn_norm[0]), w_r_hi=w_r_hi, w_r_lo=w_r_lo)
    h2, lg_t = _merge(x, ya, yb, mkv.reshape(b, m, 2 * d), params, bs["tm_merge"])
    h2 = h2.reshape(n, d)

    bias = jnp.concatenate([b_router[0], b_group[0], jnp.zeros((LANES - N_EXPERTS - N_GROUPS,), F32)])
    mi, mf, cnt = _route(lg_t, bias.reshape(LANES, 1).astype(F32), bs["tm_route"])
    tb = bs["tb"]
    counts = cnt[:, 0].astype(I32)
    padded = (counts + tb - 1) // tb * tb
    pad_end = jnp.cumsum(padded)
    pad_start = (pad_end - padded).astype(I32)
    n_blocks = (2 * n) // tb + N_EXPERTS
    block_expert = jnp.minimum(
        jnp.searchsorted(pad_end, jnp.arange(n_blocks, dtype=I32) * tb, side="right"), N_EXPERTS - 1).astype(I32)
    n_used = (pad_end[-1] // tb).astype(I32).reshape(1)

    xs = _dispatch(pad_start, mi, h2, n_blocks * tb, bs["tm_disp"])
    y = _experts(block_expert, n_used, xs, row(ffn_norm[0]), w1[0].astype(BF16), w3[0].astype(BF16),
                 w2[0].astype(BF16), tb)
    out = _combine(pad_start, mi, mf, h2, row(final_norm), y, bs["tm_comb"])
    return out.reshape(b, s, d)
```

```python
import functools

import jax
import jax.numpy as jnp
from jax import lax
from jax.experimental import pallas as pl
from jax.experimental.pallas import tpu as pltpu

F32 = jnp.float32
BF16 = jnp.bfloat16
I32 = jnp.int32

EPS = 1e-6
CHUNK_SHIFT = 6
H_A, DH_A = 4, 64
H_B, DH_B = 8, 64
H_I, D_I = 4, 64
TOPK_MAX = 256
H_X = 4
N_GROUPS, EXP_PER_GROUP = 4, 8
N_EXPERTS = N_GROUPS * EXP_PER_GROUP
LAMBDA_INIT = 0.8 - 0.6 * 1.0
LANES = 128
NEG = -1e30
INT_MIN = -2147483648
KEY_NEG_INF = -2139095041
VMEM_LIMIT = 56 * 1024 * 1024

PROJ_WIDTHS = (512, 512, 512, 512, 128, 128, 256, 128, 128)
IN_WIDTHS = (512, 512, 512, 512, 64, 64, 256, 64, 4)


def _rms(x, g):
    ms = jnp.mean(x * x, axis=-1, keepdims=True)
    return (x * lax.rsqrt(ms + EPS)) * g


def _dot_nt(a, b):
    return lax.dot_general(a, b, (((1,), (1,)), ((), ())), preferred_element_type=F32)


def _dot(a, b):
    return jnp.dot(a, b, preferred_element_type=F32)


def _wide(x, n):
    return x if n == 1 else jnp.concatenate([x] * n, axis=1)


def _proj_kernel(x_ref, g_ref, w_ref, *out_refs):
    xn = _rms(x_ref[...], g_ref[...]).astype(BF16)
    col = 0
    for ref in out_refs:
        n = ref.shape[-1]
        ref[...] = _dot(xn, w_ref[:, col:col + n]).astype(ref.dtype)
        col += n


def _proj(x2, gain, w_all, tm):
    n, d = x2.shape
    dts = (BF16,) * 8 + (F32,)
    return pl.pallas_call(
        _proj_kernel,
        grid=(n // tm,),
        in_specs=[pl.BlockSpec((tm, d), lambda i: (i, 0)),
                  pl.BlockSpec((1, d), lambda i: (0, 0)),
                  pl.BlockSpec(w_all.shape, lambda i: (0, 0))],
        out_specs=[pl.BlockSpec((tm, w), lambda i: (i, 0)) for w in PROJ_WIDTHS],
        out_shape=[jax.ShapeDtypeStruct((n, w), dt) for w, dt in zip(PROJ_WIDTHS, dts)],
        compiler_params=pltpu.CompilerParams(dimension_semantics=("arbitrary",),
                                             vmem_limit_bytes=VMEM_LIMIT),
        name="proj",
    )(x2, gain, w_all)


def _diff_kernel(slope_ref, lam_ref, q_ref, k_ref, v_ref, g_ref, o_ref, m_sc, l_sc, acc_sc, *, tq, tk):
    h = pl.program_id(1)
    qb = pl.program_id(2)
    slope = slope_ref[h]
    lam = lam_ref[0]
    q = q_ref[...]
    lane_k = lax.broadcasted_iota(I32, (tk, LANES), 1)
    row_t = qb * tq + lax.broadcasted_iota(I32, (tq, tk), 0)
    col_j = lax.broadcasted_iota(I32, (tq, tk), 1)
    t_chunk = row_t >> CHUNK_SHIFT
    nrep = tk // LANES

    m_sc[...] = jnp.full(m_sc.shape, NEG, F32)
    l_sc[...] = jnp.zeros(l_sc.shape, F32)
    acc_sc[...] = jnp.zeros(acc_sc.shape, F32)

    def block(kb, carry):
        off = pl.multiple_of(kb * tk, tk)
        k = k_ref[pl.ds(off, tk), :]
        v = v_ref[pl.ds(off, tk), :]
        zero = jnp.zeros_like(k)
        k_m = (jnp.where(lane_k < DH_A, k, zero), jnp.where(lane_k >= DH_A, k, zero))
        s_pos = off + col_j
        bias = slope * jnp.abs(row_t - s_pos).astype(F32)
        allowed = (s_pos >> CHUNK_SHIFT) <= t_chunk
        for m in range(2):
            s = jnp.where(allowed, _dot_nt(q, k_m[m]) - bias, NEG)
            m_prev = m_sc[m]
            m_next = jnp.maximum(m_prev, jnp.max(s, axis=1, keepdims=True))
            p = jnp.exp(s - _wide(m_next, nrep))
            alpha = jnp.exp(m_prev - m_next)
            l_sc[m] = alpha * l_sc[m] + jnp.sum(p, axis=1, keepdims=True)
            acc_sc[m] = alpha * acc_sc[m] + _dot(p.astype(BF16), v)
            m_sc[m] = m_next
        return carry

    lax.fori_loop(0, (qb * tq) // tk + tq // tk, block, 0)

    y = acc_sc[0] / l_sc[0] - lam * (acc_sc[1] / l_sc[1])
    o_ref[...] = (_rms(y, g_ref[...]) * (1.0 - LAMBDA_INIT)).astype(o_ref.dtype)


def _diff_attention(qa, ka, va, slopes, lam, subln, tq, tk):
    b, s, _ = qa.shape
    kern = functools.partial(_diff_kernel, tq=tq, tk=tk)
    smem = pl.BlockSpec(memory_space=pltpu.SMEM)
    return pl.pallas_call(
        kern,
        grid=(b, H_A, s // tq),
        in_specs=[smem, smem,
                  pl.BlockSpec((None, tq, LANES), lambda bi, h, i: (bi, i, h)),
                  pl.BlockSpec((None, s, LANES), lambda bi, h, i: (bi, 0, h)),
                  pl.BlockSpec((None, s, LANES), lambda bi, h, i: (bi, 0, h)),
                  pl.BlockSpec((1, LANES), lambda bi, h, i: (0, 0))],
        out_specs=pl.BlockSpec((None, tq, LANES), lambda bi, h, i: (bi, i, h)),
        out_shape=jax.ShapeDtypeStruct((b, s, H_A * 2 * DH_A), BF16),
        scratch_shapes=[pltpu.VMEM((2, tq, LANES), F32),
                        pltpu.VMEM((2, tq, LANES), F32),
                        pltpu.VMEM((2, tq, LANES), F32)],
        compiler_params=pltpu.CompilerParams(dimension_semantics=("arbitrary",) * 3,
                                             vmem_limit_bytes=VMEM_LIMIT),
        name="diff_attn",
    )(slopes, lam, qa, ka, va, subln)


def _dsa_kernel(slope_ref, qb_ref, kv1_ref, kv2_ref, qi_ref, kik_ref, wi_ref, o_ref,
                key_sc, m_sc, acc_sc, wrep_sc, p_sc, *, tq, tk, k_sel, idx_bits):
    qblk = pl.program_id(1)
    nvis = (qblk * tq) // tk + tq // tk
    nrep = tk // LANES
    lane_k = lax.broadcasted_iota(I32, (tk, LANES), 1)
    lane_q = lax.broadcasted_iota(I32, (tq, LANES), 1)
    row_t = qblk * tq + lax.broadcasted_iota(I32, (tq, tk), 0)
    col_j = lax.broadcasted_iota(I32, (tq, tk), 1)
    t_chunk = row_t >> CHUNK_SHIFT

    wi = wi_ref[...]
    for h in range(H_I):
        wrep_sc[h] = jnp.broadcast_to(wi[:, h:h + 1], (tq, LANES))
    qidx = qi_ref[...]

    def index_block(kb, carry):
        off = pl.multiple_of(kb * tk, tk)
        kik = kik_ref[pl.ds(off, tk), :]
        zero = jnp.zeros_like(kik)
        k_half = (jnp.where(lane_k < D_I, kik, zero), jnp.where(lane_k >= D_I, kik, zero))
        isc = jnp.zeros((tq, tk), F32)
        for h in range(H_I):
            qp = qidx[:, (h // 2) * LANES:(h // 2 + 1) * LANES]
            d = _dot_nt(qp, k_half[h % 2])
            isc = isc + _wide(wrep_sc[h], nrep) * jnp.maximum(d, 0.0)
        bits = lax.bitcast_convert_type(isc, I32)
        key = jnp.where(bits < 0, bits ^ 0x7FFFFFFF, bits)
        key = jnp.where(isc == 0.0, 0, key)
        allowed = ((off + col_j) >> CHUNK_SHIFT) <= t_chunk
        key_sc[kb] = jnp.where(allowed, key, KEY_NEG_INF)
        return carry

    lax.fori_loop(0, nvis, index_block, 0)

    def count_ge(cand):
        def body(kb, acc):
            kblk = key_sc[kb]
            for c in range(nrep):
                acc = acc + jnp.where(kblk[:, c * LANES:(c + 1) * LANES] >= cand, 1.0, 0.0)
            return acc
        acc = lax.fori_loop(0, nvis, body, jnp.zeros((tq, LANES), F32))
        return jnp.sum(acc, axis=1, keepdims=True)

    kf = float(k_sel)
    zero_i = jnp.zeros((tq, LANES), I32)
    thr = jnp.where(count_ge(zero_i) >= kf, zero_i, jnp.full((tq, LANES), INT_MIN, I32))

    def bit_step(i, t):
        cand = t + lax.shift_left(jnp.int32(1), jnp.asarray(30 - i, I32))
        return jnp.where(count_ge(cand) >= kf, cand, t)

    thr = lax.fori_loop(0, 31, bit_step, thr)

    need = kf - count_ge(thr + 1)
    surplus = jnp.where(thr == KEY_NEG_INF, 0.0, count_ge(thr) - kf)
    thr_eq = jnp.where(thr == KEY_NEG_INF, INT_MIN, thr)
    p_sc[...] = jnp.full((tq, LANES), (1 << idx_bits) - 1, I32)

    @pl.when(jnp.max(surplus) > 0.0)
    def _():
        def count_eq_before(cand):
            def body(kb, acc):
                kblk = key_sc[kb]
                for c in range(nrep):
                    idx = kb * tk + c * LANES + lane_q
                    hit = jnp.where(idx < cand, 1.0, 0.0)
                    acc = acc + jnp.where(kblk[:, c * LANES:(c + 1) * LANES] == thr_eq, hit, 0.0)
                return acc
            acc = lax.fori_loop(0, nvis, body, jnp.zeros((tq, LANES), F32))
            return jnp.sum(acc, axis=1, keepdims=True)

        def idx_step(i, p):
            cand = p + lax.shift_left(jnp.int32(1), jnp.asarray(idx_bits - 1 - i, I32))
            return jnp.where(count_eq_before(cand) < need, cand, p)

        p_sc[...] = lax.fori_loop(0, idx_bits, idx_step, zero_i)

    last_eq = p_sc[...]

    m_sc[...] = jnp.full(m_sc.shape, NEG, F32)
    acc_sc[...] = jnp.zeros(acc_sc.shape, F32)
    qb = qb_ref[...]

    def attend_block(kb, carry):
        off = pl.multiple_of(kb * tk, tk)
        kv1 = kv1_ref[pl.ds(off, tk), :]
        kv2 = kv2_ref[pl.ds(off, tk), :]
        zero = jnp.zeros_like(kv1)
        one = jnp.ones_like(kv1)
        k_half = (jnp.where(lane_k < DH_B, kv1, zero), jnp.where(lane_k >= DH_B, kv2, zero))
        v_half = (jnp.where(lane_k < DH_B, kv2, one), jnp.where(lane_k >= DH_B, kv1, one))
        key = key_sc[kb]
        s_pos = off + col_j
        keep_tie = jnp.where(s_pos <= _wide(last_eq, nrep), 0.0, NEG)
        mask = jnp.where(key > _wide(thr, nrep), 0.0,
                         jnp.where(key == _wide(thr_eq, nrep), keep_tie, NEG))
        dist = jnp.abs(row_t - s_pos).astype(F32)
        for h in range(H_B):
            qp = qb[:, (h // 2) * LANES:(h // 2 + 1) * LANES]
            s = _dot_nt(qp, k_half[h % 2]) - slope_ref[h] * dist + mask
            m_prev = m_sc[h]
            m_next = jnp.maximum(m_prev, jnp.max(s, axis=1, keepdims=True))
            p = jnp.exp(s - _wide(m_next, nrep))
            alpha = jnp.exp(m_prev - m_next)
            acc_sc[h] = alpha * acc_sc[h] + _dot(p.astype(BF16), v_half[h % 2])
            m_sc[h] = m_next
        return carry

    lax.fori_loop(0, nvis, attend_block, 0)

    for pr in range(H_B // 2):
        ae = acc_sc[2 * pr]
        ao = acc_sc[2 * pr + 1]
        num = jnp.where(lane_q < DH_B, ae, ao)
        den = jnp.where(lane_q < DH_B, pltpu.roll(ae, DH_B, 1), pltpu.roll(ao, DH_B, 1))
        o_ref[:, pr * LANES:(pr + 1) * LANES] = (num / den).astype(o_ref.dtype)


def _dsa_attention(qb, kv1, kv2, qi, kik, wi, slopes, k_sel, tq, tk):
    b, s, _ = qb.shape
    idx_bits = max(1, (s - 1).bit_length())
    kern = functools.partial(_dsa_kernel, tq=tq, tk=tk, k_sel=k_sel, idx_bits=idx_bits)
    tile = lambda w: pl.BlockSpec((None, tq, w), lambda bi, i: (bi, i, 0))
    full = lambda w: pl.BlockSpec((None, s, w), lambda bi, i: (bi, 0, 0))
    return pl.pallas_call(
        kern,
        grid=(b, s // tq),
        in_specs=[pl.BlockSpec(memory_space=pltpu.SMEM),
                  tile(H_B * DH_B), full(LANES), full(LANES), tile(H_I * D_I), full(LANES), tile(LANES)],
        out_specs=tile(H_B * DH_B),
        out_shape=jax.ShapeDtypeStruct((b, s, H_B * DH_B), BF16),
        scratch_shapes=[pltpu.VMEM((s // tk, tq, tk), I32),
                        pltpu.VMEM((H_B, tq, LANES), F32),
                        pltpu.VMEM((H_B, tq, LANES), F32),
                        pltpu.VMEM((H_I, tq, LANES), F32),
                        pltpu.VMEM((tq, LANES), I32)],
        compiler_params=pltpu.CompilerParams(dimension_semantics=("arbitrary",) * 2,
                                             vmem_limit_bytes=VMEM_LIMIT),
        name="dsa_attn",
    )(slopes, qb, kv1, kv2, qi, kik, wi)


def _memkv_kernel(m_ref, g_ref, w_ref, o_ref):
    mn = _rms(m_ref[...], g_ref[...]).astype(BF16)
    o_ref[...] = _dot(mn, w_ref[...]).astype(o_ref.dtype)


def _memkv(mem2, gain, w_ckv, tm):
    n, d = mem2.shape
    return pl.pallas_call(
        _memkv_kernel,
        grid=(n // tm,),
        in_specs=[pl.BlockSpec((tm, d), lambda i: (i, 0)),
                  pl.BlockSpec((1, d), lambda i: (0, 0)),
                  pl.BlockSpec(w_ckv.shape, lambda i: (0, 0))],
        out_specs=pl.BlockSpec((tm, w_ckv.shape[1]), lambda i: (i, 0)),
        out_shape=jax.ShapeDtypeStruct((n, w_ckv.shape[1]), BF16),
        compiler_params=pltpu.CompilerParams(dimension_semantics=("arbitrary",),
                                             vmem_limit_bytes=VMEM_LIMIT),
        name="memkv",
    )(mem2, gain, w_ckv)


def _merge_kernel(x_ref, ya_ref, yb_ref, mkv_ref, mixg_ref, wgate_ref, bgate_ref, wa_ref, wb_ref, wout_ref,
                  crossg_ref, wcq_ref, wco_ref, ffng_ref, wrh_ref, wrl_ref, h2_ref, lg_ref):
    d = x_ref.shape[-1]
    dh = d // H_X
    x = x_ref[...]
    xn = _rms(x, mixg_ref[...]).astype(BF16)
    gates = jax.nn.sigmoid(_dot(xn, wgate_ref[...]) + bgate_ref[...])
    merged = gates[:, :d] * _dot(ya_ref[...], wa_ref[...]) + gates[:, d:] * _dot(yb_ref[...], wb_ref[...])
    h1 = x + _dot(merged.astype(BF16), wout_ref[...])

    q = _dot(_rms(h1, crossg_ref[...]).astype(BF16), wcq_ref[...]).astype(BF16)
    heads = []
    for h in range(H_X):
        k = mkv_ref[:, h * dh:(h + 1) * dh]
        v = mkv_ref[:, d + h * dh:d + (h + 1) * dh]
        s = _dot_nt(q[:, h * dh:(h + 1) * dh], k)
        e = jnp.exp(s - jnp.max(s, axis=1, keepdims=True))
        p = e / jnp.sum(e, axis=1, keepdims=True)
        heads.append(_dot(p.astype(BF16), v).astype(BF16))
    h2 = h1 + _dot(jnp.concatenate(heads, axis=1), wco_ref[...])
    h2_ref[...] = h2

    f = _rms(h2, ffng_ref[...])
    f_hi = f.astype(BF16)
    f_lo = (f - f_hi.astype(F32)).astype(BF16)
    w_hi = wrh_ref[...]
    lg_ref[...] = _dot_nt(w_hi, f_hi) + _dot_nt(w_hi, f_lo) + _dot_nt(wrl_ref[...], f_hi)


def _merge(x3, ya, yb, mkv, p, tm):
    b, s, d = x3.shape
    m = mkv.shape[1]
    tok = lambda w: pl.BlockSpec((None, tm, w), lambda bi, i: (bi, i, 0))
    const = lambda a: pl.BlockSpec(a.shape, lambda bi, i: (0,) * a.ndim)
    consts = (p["mix_g"], p["w_gate"], p["b_gate"], p["w_a"], p["w_b"], p["w_out"], p["cross_g"],
              p["w_cq"], p["w_co"], p["ffn_g"], p["w_r_hi"], p["w_r_lo"])
    return pl.pallas_call(
        _merge_kernel,
        grid=(b, s // tm),
        in_specs=[tok(d), tok(ya.shape[-1]), tok(yb.shape[-1]),
                  pl.BlockSpec((None, m, 2 * d), lambda bi, i: (bi, 0, 0))] + [const(a) for a in consts],
        out_specs=[tok(d), pl.BlockSpec((LANES, tm), lambda bi, i: (0, bi * (s // tm) + i))],
        out_shape=[jax.ShapeDtypeStruct((b, s, d), F32), jax.ShapeDtypeStruct((LANES, b * s), F32)],
        compiler_params=pltpu.CompilerParams(dimension_semantics=("arbitrary",) * 2,
                                             vmem_limit_bytes=VMEM_LIMIT),
        name="merge",
    )(x3, ya, yb, mkv, *consts)


def _route_kernel(lg_ref, bias_ref, mi_ref, mf_ref, cnt_ref, carry_sc, *, tm):
    @pl.when(pl.program_id(0) == 0)
    def _():
        carry_sc[...] = jnp.zeros(carry_sc.shape, F32)

    lg = lg_ref[...] + bias_ref[...]
    e = lg[0:N_EXPERTS]
    g = lg[N_EXPERTS:N_EXPERTS + 8]
    row_g = lax.broadcasted_iota(I32, (8, tm), 0)
    g = jnp.where(row_g < N_GROUPS, g, -jnp.inf)
    gmax = jnp.max(g, axis=0, keepdims=True)
    g_idx = jnp.min(jnp.where(g == gmax, row_g, N_GROUPS), axis=0, keepdims=True)
    p_g = 1.0 / jnp.sum(jnp.exp(g - gmax), axis=0, keepdims=True)

    row_e = lax.broadcasted_iota(I32, (N_EXPERTS, tm), 0)
    in_grp = (row_e >> 3) == g_idx
    emax = jnp.max(jnp.where(in_grp, e, -jnp.inf), axis=0, keepdims=True)
    ex = jnp.where(in_grp, jnp.exp(jnp.where(in_grp, e - emax, 0.0)), 0.0)
    probs = ex / jnp.sum(ex, axis=0, keepdims=True)
    big = N_EXPERTS
    p1 = jnp.max(probs, axis=0, keepdims=True)
    i1 = jnp.min(jnp.where(in_grp, jnp.where(probs == p1, row_e, big), big), axis=0, keepdims=True)
    probs2 = jnp.where(in_grp, jnp.where(row_e == i1, -1.0, probs), -1.0)
    p2 = jnp.max(probs2, axis=0, keepdims=True)
    i2 = jnp.min(jnp.where(probs2 == p2, row_e, big), axis=0, keepdims=True)
    denom = p1 + p2
    gate1 = p_g * p1 / denom
    gate2 = p_g * p2 / denom

    used = jnp.where(row_e == i1, 1.0, jnp.where(row_e == i2, 1.0, 0.0))
    r = lax.broadcasted_iota(I32, (tm, tm), 0)
    c = lax.broadcasted_iota(I32, (tm, tm), 1)
    before = jnp.where(r < c, 1.0, 0.0).astype(BF16)
    excl = _dot(used.astype(BF16), before) + carry_sc[:, 0:1]
    rank1 = jnp.sum(jnp.where(row_e == i1, excl, 0.0), axis=0, keepdims=True)
    rank2 = jnp.sum(jnp.where(row_e == i2, excl, 0.0), axis=0, keepdims=True)
    carry_sc[...] = carry_sc[...] + jnp.sum(used, axis=1, keepdims=True)
    cnt_ref[...] = carry_sc[...]

    mi_ref[...] = jnp.zeros(mi_ref.shape, I32)
    mi_ref[0:1, :] = i1
    mi_ref[1:2, :] = i2
    mi_ref[2:3, :] = rank1.astype(I32)
    mi_ref[3:4, :] = rank2.astype(I32)
    mf_ref[...] = jnp.zeros(mf_ref.shape, F32)
    mf_ref[0:1, :] = gate1
    mf_ref[1:2, :] = gate2


def _route(lg_t, bias, tm):
    n = lg_t.shape[1]
    kern = functools.partial(_route_kernel, tm=tm)
    return pl.pallas_call(
        kern,
        grid=(n // tm,),
        in_specs=[pl.BlockSpec((LANES, tm), lambda i: (0, i)),
                  pl.BlockSpec((LANES, 1), lambda i: (0, 0))],
        out_specs=[pl.BlockSpec((8, tm), lambda i: (0, i)),
                   pl.BlockSpec((8, tm), lambda i: (0, i)),
                   pl.BlockSpec((N_EXPERTS, LANES), lambda i: (0, 0))],
        out_shape=[jax.ShapeDtypeStruct((8, n), I32), jax.ShapeDtypeStruct((8, n), F32),
                   jax.ShapeDtypeStruct((N_EXPERTS, LANES), F32)],
        scratch_shapes=[pltpu.VMEM((N_EXPERTS, LANES), F32)],
        compiler_params=pltpu.CompilerParams(dimension_semantics=("arbitrary",),
                                             vmem_limit_bytes=VMEM_LIMIT),
        name="route",
    )(lg_t, bias)


ROW_DMA_UNROLL = 8


def _dispatch_kernel(dest_ref, h2_ref, xs_in_hbm, xs_hbm, sem, *, tm):
    del xs_in_hbm

    def row_copy(t, dst_row):
        return pltpu.make_async_copy(h2_ref.at[pl.ds(t, 1)], xs_hbm.at[pl.ds(dst_row, 1)], sem)

    def issue(t, carry):
        for j in range(2):
            row_copy(t, dest_ref[j, t]).start()
        return carry

    def drain(t, carry):
        for j in range(2):
            row_copy(0, 0).wait()
        return carry

    lax.fori_loop(0, tm, issue, 0, unroll=ROW_DMA_UNROLL)
    lax.fori_loop(0, tm, drain, 0, unroll=ROW_DMA_UNROLL)


def _dispatch(dest, h2, n_slots, tm):
    n, d = h2.shape
    kern = functools.partial(_dispatch_kernel, tm=tm)
    xs0 = jnp.zeros((n_slots, d), F32)
    return pl.pallas_call(
        kern,
        grid=(n // tm,),
        in_specs=[pl.BlockSpec((8, tm), lambda i: (0, i), memory_space=pltpu.SMEM),
                  pl.BlockSpec((tm, d), lambda i: (i, 0)),
                  pl.BlockSpec(memory_space=pl.ANY)],
        out_specs=pl.BlockSpec(memory_space=pl.ANY),
        out_shape=jax.ShapeDtypeStruct((n_slots, d), F32),
        scratch_shapes=[pltpu.SemaphoreType.DMA(())],
        input_output_aliases={2: 0},
        compiler_params=pltpu.CompilerParams(dimension_semantics=("arbitrary",),
                                             has_side_effects=True, vmem_limit_bytes=VMEM_LIMIT),
        name="dispatch",
    )(dest, h2, xs0)


def _expert_kernel(be_ref, nu_ref, x_ref, g_ref, w1_ref, w3_ref, w2_ref, y_ref):
    del be_ref

    @pl.when(pl.program_id(0) < nu_ref[0])
    def _():
        xb = _rms(x_ref[...], g_ref[...]).astype(BF16)
        a = _dot(xb, w1_ref[...])
        hb = (a * jax.nn.sigmoid(a)) * _dot(xb, w3_ref[...])
        y_ref[...] = _dot(hb.astype(BF16), w2_ref[...])

    @pl.when(pl.program_id(0) >= nu_ref[0])
    def _():
        y_ref[...] = jnp.zeros(y_ref.shape, y_ref.dtype)


def _experts(block_expert, n_used, xs, gain, w1, w3, w2, tb):
    n_slots, d = xs.shape
    de = w1.shape[-1]
    nb = n_slots // tb
    row = lambda i, be, nu: (i, 0)
    grid_spec = pltpu.PrefetchScalarGridSpec(
        num_scalar_prefetch=2,
        grid=(nb,),
        in_specs=[pl.BlockSpec((tb, d), row),
                  pl.BlockSpec((1, d), lambda i, be, nu: (0, 0)),
                  pl.BlockSpec((None, d, de), lambda i, be, nu: (be[i], 0, 0)),
                  pl.BlockSpec((None, d, de), lambda i, be, nu: (be[i], 0, 0)),
                  pl.BlockSpec((None, de, d), lambda i, be, nu: (be[i], 0, 0))],
        out_specs=pl.BlockSpec((tb, d), row),
    )
    return pl.pallas_call(
        _expert_kernel,
        grid_spec=grid_spec,
        out_shape=jax.ShapeDtypeStruct((n_slots, d), F32),
        compiler_params=pltpu.CompilerParams(dimension_semantics=("arbitrary",),
                                             vmem_limit_bytes=VMEM_LIMIT),
        name="experts",
    )(block_expert, n_used, xs, gain, w1, w3, w2)


def _combine_kernel(dest_ref, mf_ref, h2_ref, g_ref, y_hbm, o_ref, ybuf, sem, *, tm):
    def row_copy(src_row, j, t):
        return pltpu.make_async_copy(y_hbm.at[pl.ds(src_row, 1)], ybuf.at[j, pl.ds(t, 1)], sem)

    def issue(t, carry):
        for j in range(2):
            row_copy(dest_ref[j, t], j, t).start()
        return carry

    def drain(t, carry):
        for j in range(2):
            row_copy(0, j, 0).wait()
        return carry

    lax.fori_loop(0, tm, issue, 0, unroll=ROW_DMA_UNROLL)
    lax.fori_loop(0, tm, drain, 0, unroll=ROW_DMA_UNROLL)

    gates = jnp.concatenate([mf_ref[...], jnp.zeros((LANES - 8, tm), F32)], axis=0)
    gt = jnp.transpose(gates)
    h = h2_ref[...] + gt[:, 0:1] * ybuf[0] + gt[:, 1:2] * ybuf[1]
    o_ref[...] = _rms(h, g_ref[...])


def _combine(dest, mf, h2, gain, y, tm):
    n, d = h2.shape
    kern = functools.partial(_combine_kernel, tm=tm)
    return pl.pallas_call(
        kern,
        grid=(n // tm,),
        in_specs=[pl.BlockSpec((8, tm), lambda i: (0, i), memory_space=pltpu.SMEM),
                  pl.BlockSpec((8, tm), lambda i: (0, i)),
                  pl.BlockSpec((tm, d), lambda i: (i, 0)),
                  pl.BlockSpec((1, d), lambda i: (0, 0)),
                  pl.BlockSpec(memory_space=pl.ANY)],
        out_specs=pl.BlockSpec((tm, d), lambda i: (i, 0)),
        out_shape=jax.ShapeDtypeStruct((n, d), F32),
        scratch_shapes=[pltpu.VMEM((2, tm, d), F32), pltpu.SemaphoreType.DMA(())],
        compiler_params=pltpu.CompilerParams(dimension_semantics=("arbitrary",),
                                             vmem_limit_bytes=VMEM_LIMIT),
        name="combine",
    )(dest, mf, h2, gain, y)


def _alibi_slopes(n):
    return jnp.asarray([2.0 ** (-8.0 * (i + 1) / n) for i in range(n)], dtype=F32)


def _prep_w_in(w_in):
    splits = []
    acc = 0
    for w in IN_WIDTHS[:-1]:
        acc += w
        splits.append(acc)
    qa, ka, va, qb, kb, vb, qi, ki, wi = jnp.split(w_in, splits, axis=1)
    pad = jnp.zeros((w_in.shape[0], LANES - IN_WIDTHS[-1]), w_in.dtype)
    cols = [qa * DH_A ** -0.5, ka, va, qb * DH_B ** -0.5, kb, vb, vb, kb, qi, ki, ki,
            wi * (H_I ** -0.5 * D_I ** -0.5), pad]
    return jnp.concatenate(cols, axis=1).astype(BF16)


def _prep_router(w_group, w_router):
    d = w_group.shape[0]
    w = jnp.concatenate([w_router, w_group, jnp.zeros((d, LANES - N_EXPERTS - N_GROUPS), F32)], axis=1).T
    hi = w.astype(BF16)
    lo = (w - hi.astype(F32)).astype(BF16)
    return hi, lo


def _block_sizes(s):
    tq = min(256, s)
    return dict(tm_proj=512, tq=tq, tk=tq, tm_merge=min(256, s), tm_route=512, tm_disp=512, tb=256, tm_comb=256)


def kernel(x, mem, mix_norm, w_in, lam_q1, lam_k1, lam_q2, lam_k2, diff_subln, w_branch_a, w_branch_b, w_gate,
           b_gate, w_out, cross_norm, mem_norm, w_cq, w_ckv, w_co, ffn_norm, w_group, b_group, w_router,
           b_router, w1, w3, w2, final_norm):
    b, s, d = x.shape
    n = b * s
    m = mem.shape[1]
    bs = _block_sizes(s)
    k_sel = min(TOPK_MAX, s // 4)
    row = lambda v: v.reshape(1, -1).astype(F32)

    outs = _proj(x.reshape(n, d), row(mix_norm[0]), _prep_w_in(w_in[0]), bs["tm_proj"])
    qa, ka, va, qb, kv1, kv2, qi, kik, wi = [o.reshape(b, s, -1) for o in outs]
    lam = (jnp.exp(jnp.sum(lam_q1[0].astype(F32) * lam_k1[0].astype(F32)))
           - jnp.exp(jnp.sum(lam_q2[0].astype(F32) * lam_k2[0].astype(F32))) + LAMBDA_INIT).reshape(1)
    ya = _diff_attention(qa, ka, va, _alibi_slopes(H_A), lam, row(diff_subln[0]), bs["tq"], bs["tk"])
    yb = _dsa_attention(qb, kv1, kv2, qi, kik, wi, _alibi_slopes(H_B), k_sel, bs["tq"], bs["tk"])

    mkv = _memkv(mem.reshape(b * m, d), row(mem_norm[0]), w_ckv[0].astype(BF16), min(512, b * m))
    w_r_hi, w_r_lo = _prep_router(w_group[0], w_router[0])
    params = dict(mix_g=row(mix_norm[0]), w_gate=w_gate[0].astype(BF16), b_gate=row(b_gate[0]),
                  w_a=w_branch_a[0].astype(BF16), w_b=w_branch_b[0].astype(BF16), w_out=w_out[0].astype(BF16),
                  cross_g=row(cross_norm[0]), w_cq=(w_cq[0] * (d // H_X) ** -0.5).astype(BF16),
                  w_co=w_co[0].astype(BF16), ffn_g=row(ffn_norm[0]), w_r_hi=w_r_hi, w_r_lo=w_r_lo)
    h2, lg_t = _merge(x, ya, yb, mkv.reshape(b, m, 2 * d), params, bs["tm_merge"])
    h2 = h2.reshape(n, d)

    bias = jnp.concatenate([b_router[0], b_group[0], jnp.zeros((LANES - N_EXPERTS - N_GROUPS,), F32)])
    mi, mf, cnt = _route(lg_t, bias.reshape(LANES, 1).astype(F32), bs["tm_route"])
    tb = bs["tb"]
    counts = cnt[:, 0].astype(I32)
    padded = (counts + tb - 1) // tb * tb
    pad_end = jnp.cumsum(padded)
    pad_start = (pad_end - padded).astype(I32)
    n_blocks = (2 * n) // tb + N_EXPERTS
    blk_first = jnp.arange(n_blocks, dtype=I32) * tb
    block_expert = jnp.minimum(jnp.sum((pad_end[None, :] <= blk_first[:, None]).astype(I32), axis=1), N_EXPERTS - 1)
    n_used = (pad_end[-1] // tb).astype(I32).reshape(1)
    start_of = jnp.sum(jnp.where(mi[0:2][None] == jnp.arange(N_EXPERTS, dtype=I32)[:, None, None],
                                 pad_start[:, None, None], 0), axis=0)
    dest = jnp.concatenate([start_of + mi[2:4], jnp.zeros((6, n), I32)], axis=0)

    xs = _dispatch(dest, h2, n_blocks * tb, bs["tm_disp"])
    y = _experts(block_expert, n_used, xs, row(ffn_norm[0]), w1[0].astype(BF16), w3[0].astype(BF16),
                 w2[0].astype(BF16), tb)
    out = _combine(dest, mf, h2, row(final_norm), y, bs["tm_comb"])
    return out.reshape(b, s, d)
```

```python
import functools

import jax
import jax.numpy as jnp
from jax import lax
from jax.experimental import pallas as pl
from jax.experimental.pallas import tpu as pltpu

F32 = jnp.float32
BF16 = jnp.bfloat16
I32 = jnp.int32

EPS = 1e-6
CHUNK_SHIFT = 6
H_A, DH_A = 4, 64
H_B, DH_B = 8, 64
H_I, D_I = 4, 64
TOPK_MAX = 256
H_X = 4
N_GROUPS, EXP_PER_GROUP = 4, 8
N_EXPERTS = N_GROUPS * EXP_PER_GROUP
LAMBDA_INIT = 0.8 - 0.6 * 1.0
LANES = 128
NEG = -1e30
INT_MIN = -2147483648
KEY_NEG_INF = -2139095041
VMEM_LIMIT = 56 * 1024 * 1024

LOG2E = 1.4426950408889634
TK = 256
WIT_ROWS = 16
IN_WIDTHS = (512, 512, 512, 512, 64, 64, 256, 64, 4)


def _rms(x, g):
    ms = jnp.mean(x * x, axis=-1, keepdims=True)
    return (x * lax.rsqrt(ms + EPS)) * g


def _dot_nt(a, b):
    return lax.dot_general(a, b, (((1,), (1,)), ((), ())), preferred_element_type=F32)


def _dot(a, b):
    return jnp.dot(a, b, preferred_element_type=F32)


def _proj_kernel(x_ref, g_ref, wn_ref, wt_ref, qa_ref, ka_ref, qb_ref, kb_ref, qi_ref, kik_ref,
                 vat_ref, vbt_ref, wit_ref):
    xn = _rms(x_ref[...], g_ref[...]).astype(BF16)
    col = 0
    for ref in (qa_ref, ka_ref, qb_ref, kb_ref, qi_ref, kik_ref):
        n = ref.shape[-1]
        ref[...] = _dot(xn, wn_ref[:, col:col + n]).astype(ref.dtype)
        col += n
    row = 0
    for ref in (vat_ref, vbt_ref):
        r = ref.shape[1]
        yt = _dot_nt(wt_ref[row:row + r, :], xn)
        for c in range(ref.shape[0]):
            ref[c] = yt[:, c * TK:(c + 1) * TK].astype(ref.dtype)
        row += r
    wit_ref[...] = _dot_nt(wt_ref[row:row + wit_ref.shape[0], :], xn)


def _proj(x2, gain, w_n, w_t, tm):
    n, d = x2.shape
    kb = tm // TK
    tok = lambda w: pl.BlockSpec((tm, w), lambda i: (i, 0))
    widths = (H_A * 2 * DH_A, H_A * 2 * DH_A, H_B * DH_B, LANES, H_I * D_I, LANES)
    return pl.pallas_call(
        _proj_kernel,
        grid=(n // tm,),
        in_specs=[pl.BlockSpec((tm, d), lambda i: (i, 0)),
                  pl.BlockSpec((1, d), lambda i: (0, 0)),
                  pl.BlockSpec(w_n.shape, lambda i: (0, 0)),
                  pl.BlockSpec(w_t.shape, lambda i: (0, 0))],
        out_specs=[tok(w) for w in widths] + [
            pl.BlockSpec((kb, H_A * 2 * DH_A, TK), lambda i: (i, 0, 0)),
            pl.BlockSpec((kb, DH_B, TK), lambda i: (i, 0, 0)),
            pl.BlockSpec((WIT_ROWS, tm), lambda i: (0, i))],
        out_shape=[jax.ShapeDtypeStruct((n, w), BF16) for w in widths] + [
            jax.ShapeDtypeStruct((n // TK, H_A * 2 * DH_A, TK), BF16),
            jax.ShapeDtypeStruct((n // TK, DH_B, TK), BF16),
            jax.ShapeDtypeStruct((WIT_ROWS, n), F32)],
        compiler_params=pltpu.CompilerParams(dimension_semantics=("arbitrary",),
                                             vmem_limit_bytes=VMEM_LIMIT),
        name="proj",
    )(x2, gain, w_n, w_t)


def _aug_query(sig_ref, h, tq):
    lane = lax.broadcasted_iota(I32, (tq, LANES), 1)
    i = lax.broadcasted_iota(I32, (tq, LANES), 0).astype(F32)
    s1, s2, s3 = sig_ref[3 * h], sig_ref[3 * h + 1], sig_ref[3 * h + 2]
    c = ((s1 + s2) + s3) * i
    c1 = c.astype(BF16).astype(F32)
    c2 = (c - c1).astype(BF16).astype(F32)
    c3 = (c - c1) - c2
    out = jnp.zeros((tq, LANES), F32)
    for n, v in enumerate((c1, c2, c3, s1, s2, s3, -s1, -s2, -s3)):
        out = jnp.where(lane == n, v, out)
    return out.astype(BF16)


def _aug_key(tk, off):
    lane = lax.broadcasted_iota(I32, (tk, LANES), 1)
    j = lax.broadcasted_iota(I32, (tk, LANES), 0).astype(F32)
    base = jnp.where(lane < 3, -1.0, jnp.where(lane < 6, j, jnp.where(lane < 9, off, 0.0)))
    return base.astype(BF16)


def _sigma(sig_ref, h):
    return (sig_ref[3 * h] + sig_ref[3 * h + 1]) + sig_ref[3 * h + 2]


def _softmax_step(r, st, vt, m_sc, l_sc, acc_sc):
    m_prev = m_sc[r][0:1, :]
    m_next = jnp.maximum(m_prev, jnp.max(st, axis=0, keepdims=True))
    p = jnp.exp2(st - m_next)
    alpha = jnp.exp2(m_prev - m_next)
    l_next = alpha * l_sc[r][0:1, :] + jnp.sum(p, axis=0, keepdims=True)
    acc_sc[r] = alpha * acc_sc[r] + _dot(vt, p.astype(BF16))
    m_sc[r] = jnp.broadcast_to(m_next, m_sc.shape[1:])
    l_sc[r] = jnp.broadcast_to(l_next, l_sc.shape[1:])


def _diff_kernel(sig_ref, lam_ref, q_ref, k_ref, vt_ref, g_ref, o_ref, qaug_sc, m_sc, l_sc, acc_sc, *, tq, tk):
    qblk = pl.program_id(1)
    lam = lam_ref[0]
    lane_k = lax.broadcasted_iota(I32, (tk, LANES), 1)
    for h in range(H_A):
        qaug_sc[h] = jnp.concatenate([q_ref[:, h * LANES:(h + 1) * LANES], _aug_query(sig_ref, h, tq)], axis=1)
    m_sc[...] = jnp.full(m_sc.shape, NEG, F32)
    l_sc[...] = jnp.zeros(l_sc.shape, F32)
    acc_sc[...] = jnp.zeros(acc_sc.shape, F32)

    def block(kb, diag):
        off = pl.multiple_of(kb * tk, tk)
        k = k_ref[pl.ds(off, tk), :]
        rel = qblk * tq - kb * tk
        augk = _aug_key(tk, rel.astype(F32))
        if diag:
            jj = lax.broadcasted_iota(I32, (tk, tq), 0)
            ii = lax.broadcasted_iota(I32, (tk, tq), 1)
            allowed = ((off + jj) >> CHUNK_SHIFT) <= ((qblk * tq + ii) >> CHUNK_SHIFT)
            ahead = jnp.maximum(jj - ii - rel, 0).astype(F32)
        for h in range(H_A):
            kh = k[:, h * LANES:(h + 1) * LANES]
            zero = jnp.zeros_like(kh)
            vt = vt_ref[kb, h * LANES:(h + 1) * LANES, :]
            qa = qaug_sc[h]
            for m in range(2):
                km = jnp.where(lane_k < DH_A if m == 0 else lane_k >= DH_A, kh, zero)
                st = _dot_nt(jnp.concatenate([km, augk], axis=1), qa)
                if diag:
                    st = jnp.where(allowed, st - (2.0 * _sigma(sig_ref, h)) * ahead, NEG)
                _softmax_step(2 * h + m, st, vt, m_sc, l_sc, acc_sc)

    def off_diagonal(kb, carry):
        block(kb, False)
        return carry

    n_before = (qblk * tq) // tk
    lax.fori_loop(0, n_before, off_diagonal, 0)
    for c in range(tq // tk):
        block(n_before + c, True)

    gain = jnp.concatenate([g_ref[...]] * (tq // LANES), axis=1)
    for h in range(H_A):
        y = acc_sc[2 * h] / l_sc[2 * h][0:1, :] - lam * (acc_sc[2 * h + 1] / l_sc[2 * h + 1][0:1, :])
        ms = jnp.mean(y * y, axis=0, keepdims=True)
        yn = (y * lax.rsqrt(ms + EPS)) * gain
        o_ref[:, h * LANES:(h + 1) * LANES] = (jnp.transpose(yn) * (1.0 - LAMBDA_INIT)).astype(o_ref.dtype)


def _diff_attention(qa, ka, vat, sig, lam, subln_t, tq, tk):
    b, s, w = qa.shape
    assert tq % tk == 0
    kern = functools.partial(_diff_kernel, tq=tq, tk=tk)
    smem = pl.BlockSpec(memory_space=pltpu.SMEM)
    return pl.pallas_call(
        kern,
        grid=(b, s // tq),
        in_specs=[smem, smem,
                  pl.BlockSpec((None, tq, w), lambda bi, i: (bi, i, 0)),
                  pl.BlockSpec((None, s, w), lambda bi, i: (bi, 0, 0)),
                  pl.BlockSpec((None, s // tk, w, tk), lambda bi, i: (bi, 0, 0, 0)),
                  pl.BlockSpec((LANES, LANES), lambda bi, i: (0, 0))],
        out_specs=pl.BlockSpec((None, tq, w), lambda bi, i: (bi, i, 0)),
        out_shape=jax.ShapeDtypeStruct((b, s, w), BF16),
        scratch_shapes=[pltpu.VMEM((H_A, tq, 2 * LANES), BF16),
                        pltpu.VMEM((2 * H_A, 8, tq), F32),
                        pltpu.VMEM((2 * H_A, 8, tq), F32),
                        pltpu.VMEM((2 * H_A, LANES, tq), F32)],
        compiler_params=pltpu.CompilerParams(dimension_semantics=("arbitrary",) * 2,
                                             vmem_limit_bytes=VMEM_LIMIT),
        name="diff_attn",
    )(sig, lam, qa, ka, vat, subln_t)


def _dsa_kernel(sig_ref, qb_ref, kb_ref, vt_ref, qi_ref, kik_ref, wit_ref, o_ref,
                qaug_sc, key_sc, m_sc, l_sc, acc_sc, cut_sc, *, tq, tk, k_sel, idx_bits):
    qblk = pl.program_id(1)
    n_before = (qblk * tq) // tk
    nvis = n_before + tq // tk
    lane_k = lax.broadcasted_iota(I32, (tk, LANES), 1)
    jj = lax.broadcasted_iota(I32, (tk, tq), 0)
    ii = lax.broadcasted_iota(I32, (tk, tq), 1)
    t_chunk = (qblk * tq + ii) >> CHUNK_SHIFT
    w_idx = wit_ref[...]
    qidx = qi_ref[...]

    def index_block(kb, carry):
        off = pl.multiple_of(kb * tk, tk)
        kik = kik_ref[pl.ds(off, tk), :]
        zero = jnp.zeros_like(kik)
        k_half = (jnp.where(lane_k < D_I, kik, zero), jnp.where(lane_k >= D_I, kik, zero))
        isc = jnp.zeros((tk, tq), F32)
        for h in range(H_I):
            d = _dot_nt(k_half[h % 2], qidx[:, (h // 2) * LANES:(h // 2 + 1) * LANES])
            isc = isc + w_idx[h:h + 1, :] * jnp.maximum(d, 0.0)
        bits = lax.bitcast_convert_type(isc, I32)
        key = jnp.where(bits < 0, bits ^ 0x7FFFFFFF, bits)
        key = jnp.where(isc == 0.0, 0, key)
        allowed = ((off + jj) >> CHUNK_SHIFT) <= t_chunk
        key_sc[kb] = jnp.where(allowed, key, KEY_NEG_INF)
        return carry

    lax.fori_loop(0, nvis, index_block, 0)

    def count_ge(cand):
        def body(kb, acc):
            hit = jnp.where(key_sc[kb] >= cand, 1.0, 0.0)
            return acc + jnp.sum(hit.reshape(tk // 8, 8, tq), axis=0)
        acc = lax.fori_loop(0, nvis, body, jnp.zeros((8, tq), F32))
        return jnp.sum(acc, axis=0, keepdims=True)

    kf = float(k_sel)
    zero_i = jnp.zeros((1, tq), I32)
    thr = jnp.where(count_ge(zero_i) >= kf, zero_i, jnp.full((1, tq), INT_MIN, I32))

    def bit_step(i, t):
        cand = t + lax.shift_left(jnp.int32(1), jnp.asarray(30 - i, I32))
        return jnp.where(count_ge(cand) >= kf, cand, t)

    thr = lax.fori_loop(0, 31, bit_step, thr)

    need = kf - count_ge(thr + 1)
    surplus = jnp.where(thr == KEY_NEG_INF, 0.0, count_ge(thr) - kf)
    thr_eq = jnp.where(thr == KEY_NEG_INF, INT_MIN, thr)
    cut_sc[...] = jnp.full(cut_sc.shape, (1 << idx_bits) - 1, I32)

    @pl.when(jnp.max(surplus) > 0.0)
    def _():
        def count_eq_before(cand):
            def body(kb, acc):
                hit = jnp.where(kb * tk + jj < cand, 1.0, 0.0)
                return acc + jnp.sum(jnp.where(key_sc[kb] == thr_eq, hit, 0.0), axis=0, keepdims=True)
            return lax.fori_loop(0, nvis, body, jnp.zeros((1, tq), F32))

        def idx_step(i, p):
            cand = p + lax.shift_left(jnp.int32(1), jnp.asarray(idx_bits - 1 - i, I32))
            return jnp.where(count_eq_before(cand) < need, cand, p)

        cut_sc[0:1, :] = lax.fori_loop(0, idx_bits, idx_step, zero_i)

    last_eq = cut_sc[0:1, :]

    for h in range(H_B):
        qaug_sc[h] = jnp.concatenate([qb_ref[:, (h // 2) * LANES:(h // 2 + 1) * LANES],
                                      _aug_query(sig_ref, h, tq)], axis=1)
    m_sc[...] = jnp.full(m_sc.shape, NEG, F32)
    l_sc[...] = jnp.zeros(l_sc.shape, F32)
    acc_sc[...] = jnp.zeros(acc_sc.shape, F32)

    def attend(kb, diag):
        off = pl.multiple_of(kb * tk, tk)
        kk = kb_ref[pl.ds(off, tk), :]
        zero = jnp.zeros_like(kk)
        rel = qblk * tq - kb * tk
        augk = _aug_key(tk, rel.astype(F32))
        k_aug = (jnp.concatenate([jnp.where(lane_k < DH_B, kk, zero), augk], axis=1),
                 jnp.concatenate([jnp.where(lane_k >= DH_B, kk, zero), augk], axis=1))
        vt = vt_ref[kb]
        key = key_sc[kb]
        keep_tie = jnp.where(off + jj <= last_eq, 0.0, NEG)
        mask = jnp.where(key > thr, 0.0, jnp.where(key == thr_eq, keep_tie, NEG))
        if diag:
            ahead = jnp.maximum(jj - ii - rel, 0).astype(F32)
        for h in range(H_B):
            st = _dot_nt(k_aug[h % 2], qaug_sc[h]) + mask
            if diag:
                st = st - (2.0 * _sigma(sig_ref, h)) * ahead
            _softmax_step(h, st, vt, m_sc, l_sc, acc_sc)

    def off_diagonal(kb, carry):
        attend(kb, False)
        return carry

    lax.fori_loop(0, n_before, off_diagonal, 0)
    for c in range(tq // tk):
        attend(n_before + c, True)

    yt = jnp.concatenate([acc_sc[h] / l_sc[h][0:1, :] for h in range(H_B)], axis=0)
    o_ref[...] = jnp.transpose(yt).astype(o_ref.dtype)


def _dsa_attention(qb, kb2, vbt, qi, kik, wit, sig, k_sel, tq, tk):
    b, s, w = qb.shape
    assert tq % tk == 0
    idx_bits = max(1, (s - 1).bit_length())
    kern = functools.partial(_dsa_kernel, tq=tq, tk=tk, k_sel=k_sel, idx_bits=idx_bits)
    tile = lambda c: pl.BlockSpec((None, tq, c), lambda bi, i: (bi, i, 0))
    full = lambda c: pl.BlockSpec((None, s, c), lambda bi, i: (bi, 0, 0))
    nq = s // tq
    return pl.pallas_call(
        kern,
        grid=(b, nq),
        in_specs=[pl.BlockSpec(memory_space=pltpu.SMEM),
                  tile(w), full(LANES),
                  pl.BlockSpec((None, s // tk, DH_B, tk), lambda bi, i: (bi, 0, 0, 0)),
                  tile(H_I * D_I), full(LANES),
                  pl.BlockSpec((WIT_ROWS, tq), lambda bi, i: (0, bi * nq + i))],
        out_specs=tile(w),
        out_shape=jax.ShapeDtypeStruct((b, s, w), BF16),
        scratch_shapes=[pltpu.VMEM((H_B, tq, 2 * LANES), BF16),
                        pltpu.VMEM((s // tk, tk, tq), I32),
                        pltpu.VMEM((H_B, 8, tq), F32),
                        pltpu.VMEM((H_B, 8, tq), F32),
                        pltpu.VMEM((H_B, DH_B, tq), F32),
                        pltpu.VMEM((8, tq), I32)],
        compiler_params=pltpu.CompilerParams(dimension_semantics=("arbitrary",) * 2,
                                             vmem_limit_bytes=VMEM_LIMIT),
        name="dsa_attn",
    )(sig, qb, kb2, vbt, qi, kik, wit)


def _memkv_kernel(m_ref, g_ref, w_ref, o_ref):
    mn = _rms(m_ref[...], g_ref[...]).astype(BF16)
    o_ref[...] = _dot(mn, w_ref[...]).astype(o_ref.dtype)


def _memkv(mem2, gain, w_ckv, tm):
    n, d = mem2.shape
    return pl.pallas_call(
        _memkv_kernel,
        grid=(n // tm,),
        in_specs=[pl.BlockSpec((tm, d), lambda i: (i, 0)),
                  pl.BlockSpec((1, d), lambda i: (0, 0)),
                  pl.BlockSpec(w_ckv.shape, lambda i: (0, 0))],
        out_specs=pl.BlockSpec((tm, w_ckv.shape[1]), lambda i: (i, 0)),
        out_shape=jax.ShapeDtypeStruct((n, w_ckv.shape[1]), BF16),
        compiler_params=pltpu.CompilerParams(dimension_semantics=("arbitrary",),
                                             vmem_limit_bytes=VMEM_LIMIT),
        name="memkv",
    )(mem2, gain, w_ckv)


def _merge_kernel(x_ref, ya_ref, yb_ref, mkv_ref, mixg_ref, wgate_ref, bgate_ref, wa_ref, wb_ref, wout_ref,
                  crossg_ref, wcq_ref, wco_ref, ffng_ref, wrh_ref, wrl_ref, h2_ref, lg_ref):
    d = x_ref.shape[-1]
    dh = d // H_X
    x = x_ref[...]
    xn = _rms(x, mixg_ref[...]).astype(BF16)
    gates = jax.nn.sigmoid(_dot(xn, wgate_ref[...]) + bgate_ref[...])
    merged = gates[:, :d] * _dot(ya_ref[...], wa_ref[...]) + gates[:, d:] * _dot(yb_ref[...], wb_ref[...])
    h1 = x + _dot(merged.astype(BF16), wout_ref[...])

    q = _dot(_rms(h1, crossg_ref[...]).astype(BF16), wcq_ref[...]).astype(BF16)
    heads = []
    for h in range(H_X):
        k = mkv_ref[:, h * dh:(h + 1) * dh]
        v = mkv_ref[:, d + h * dh:d + (h + 1) * dh]
        s = _dot_nt(q[:, h * dh:(h + 1) * dh], k)
        e = jnp.exp(s - jnp.max(s, axis=1, keepdims=True))
        p = e / jnp.sum(e, axis=1, keepdims=True)
        heads.append(_dot(p.astype(BF16), v).astype(BF16))
    h2 = h1 + _dot(jnp.concatenate(heads, axis=1), wco_ref[...])
    h2_ref[...] = h2

    f = _rms(h2, ffng_ref[...])
    f_hi = f.astype(BF16)
    f_lo = (f - f_hi.astype(F32)).astype(BF16)
    w_hi = wrh_ref[...]
    lg_ref[...] = _dot_nt(w_hi, f_hi) + _dot_nt(w_hi, f_lo) + _dot_nt(wrl_ref[...], f_hi)


def _merge(x3, ya, yb, mkv, p, tm):
    b, s, d = x3.shape
    m = mkv.shape[1]
    tok = lambda w: pl.BlockSpec((None, tm, w), lambda bi, i: (bi, i, 0))
    const = lambda a: pl.BlockSpec(a.shape, lambda bi, i: (0,) * a.ndim, pipeline_mode=pl.Buffered(1))
    consts = (p["mix_g"], p["w_gate"], p["b_gate"], p["w_a"], p["w_b"], p["w_out"], p["cross_g"],
              p["w_cq"], p["w_co"], p["ffn_g"], p["w_r_hi"], p["w_r_lo"])
    return pl.pallas_call(
        _merge_kernel,
        grid=(b, s // tm),
        in_specs=[tok(d), tok(ya.shape[-1]), tok(yb.shape[-1]),
                  pl.BlockSpec((None, m, 2 * d), lambda bi, i: (bi, 0, 0))] + [const(a) for a in consts],
        out_specs=[tok(d), pl.BlockSpec((LANES, tm), lambda bi, i: (0, bi * (s // tm) + i))],
        out_shape=[jax.ShapeDtypeStruct((b, s, d), F32), jax.ShapeDtypeStruct((LANES, b * s), F32)],
        compiler_params=pltpu.CompilerParams(dimension_semantics=("arbitrary",) * 2,
                                             vmem_limit_bytes=VMEM_LIMIT),
        name="merge",
    )(x3, ya, yb, mkv, *consts)


def _route_kernel(lg_ref, bias_ref, mi_ref, mf_ref, cnt_ref, carry_sc, *, tm):
    @pl.when(pl.program_id(0) == 0)
    def _():
        carry_sc[...] = jnp.zeros(carry_sc.shape, F32)

    lg = lg_ref[...] + bias_ref[...]
    e = lg[0:N_EXPERTS]
    g = lg[N_EXPERTS:N_EXPERTS + 8]
    row_g = lax.broadcasted_iota(I32, (8, tm), 0)
    g = jnp.where(row_g < N_GROUPS, g, -jnp.inf)
    gmax = jnp.max(g, axis=0, keepdims=True)
    g_idx = jnp.min(jnp.where(g == gmax, row_g, N_GROUPS), axis=0, keepdims=True)
    p_g = 1.0 / jnp.sum(jnp.exp(g - gmax), axis=0, keepdims=True)

    row_e = lax.broadcasted_iota(I32, (N_EXPERTS, tm), 0)
    in_grp = (row_e >> 3) == g_idx
    emax = jnp.max(jnp.where(in_grp, e, -jnp.inf), axis=0, keepdims=True)
    ex = jnp.where(in_grp, jnp.exp(jnp.where(in_grp, e - emax, 0.0)), 0.0)
    probs = ex / jnp.sum(ex, axis=0, keepdims=True)
    big = N_EXPERTS
    p1 = jnp.max(probs, axis=0, keepdims=True)
    i1 = jnp.min(jnp.where(in_grp, jnp.where(probs == p1, row_e, big), big), axis=0, keepdims=True)
    probs2 = jnp.where(in_grp, jnp.where(row_e == i1, -1.0, probs), -1.0)
    p2 = jnp.max(probs2, axis=0, keepdims=True)
    i2 = jnp.min(jnp.where(probs2 == p2, row_e, big), axis=0, keepdims=True)
    denom = p1 + p2
    gate1 = p_g * p1 / denom
    gate2 = p_g * p2 / denom

    used = jnp.where(row_e == i1, 1.0, jnp.where(row_e == i2, 1.0, 0.0))
    r = lax.broadcasted_iota(I32, (tm, tm), 0)
    c = lax.broadcasted_iota(I32, (tm, tm), 1)
    before = jnp.where(r < c, 1.0, 0.0).astype(BF16)
    excl = _dot(used.astype(BF16), before) + carry_sc[:, 0:1]
    rank1 = jnp.sum(jnp.where(row_e == i1, excl, 0.0), axis=0, keepdims=True)
    rank2 = jnp.sum(jnp.where(row_e == i2, excl, 0.0), axis=0, keepdims=True)
    carry_sc[...] = carry_sc[...] + jnp.sum(used, axis=1, keepdims=True)
    cnt_ref[...] = carry_sc[...]

    mi_ref[...] = jnp.zeros(mi_ref.shape, I32)
    mi_ref[0:1, :] = i1
    mi_ref[1:2, :] = i2
    mi_ref[2:3, :] = rank1.astype(I32)
    mi_ref[3:4, :] = rank2.astype(I32)
    mf_ref[...] = jnp.zeros(mf_ref.shape, F32)
    mf_ref[0:1, :] = gate1
    mf_ref[1:2, :] = gate2


def _route(lg_t, bias, tm):
    n = lg_t.shape[1]
    kern = functools.partial(_route_kernel, tm=tm)
    return pl.pallas_call(
        kern,
        grid=(n // tm,),
        in_specs=[pl.BlockSpec((LANES, tm), lambda i: (0, i)),
                  pl.BlockSpec((LANES, 1), lambda i: (0, 0))],
        out_specs=[pl.BlockSpec((8, tm), lambda i: (0, i)),
                   pl.BlockSpec((8, tm), lambda i: (0, i)),
                   pl.BlockSpec((N_EXPERTS, LANES), lambda i: (0, 0))],
        out_shape=[jax.ShapeDtypeStruct((8, n), I32), jax.ShapeDtypeStruct((8, n), F32),
                   jax.ShapeDtypeStruct((N_EXPERTS, LANES), F32)],
        scratch_shapes=[pltpu.VMEM((N_EXPERTS, LANES), F32)],
        compiler_params=pltpu.CompilerParams(dimension_semantics=("arbitrary",),
                                             vmem_limit_bytes=VMEM_LIMIT),
        name="route",
    )(lg_t, bias)


ROW_DMA_UNROLL = 8


def _dispatch_kernel(dest_ref, h2_ref, xs_in_hbm, xs_hbm, sem, *, tm):
    del xs_in_hbm

    def row_copy(t, dst_row):
        return pltpu.make_async_copy(h2_ref.at[pl.ds(t, 1)], xs_hbm.at[pl.ds(dst_row, 1)], sem)

    def issue(t, carry):
        for j in range(2):
            row_copy(t, dest_ref[j, t]).start()
        return carry

    def drain(t, carry):
        for j in range(2):
            row_copy(0, 0).wait()
        return carry

    lax.fori_loop(0, tm, issue, 0, unroll=ROW_DMA_UNROLL)
    lax.fori_loop(0, tm, drain, 0, unroll=ROW_DMA_UNROLL)


def _dispatch(dest, h2, n_slots, tm):
    n, d = h2.shape
    kern = functools.partial(_dispatch_kernel, tm=tm)
    xs0 = jnp.zeros((n_slots, d), F32)
    return pl.pallas_call(
        kern,
        grid=(n // tm,),
        in_specs=[pl.BlockSpec((8, tm), lambda i: (0, i), memory_space=pltpu.SMEM),
                  pl.BlockSpec((tm, d), lambda i: (i, 0)),
                  pl.BlockSpec(memory_space=pl.ANY)],
        out_specs=pl.BlockSpec(memory_space=pl.ANY),
        out_shape=jax.ShapeDtypeStruct((n_slots, d), F32),
        scratch_shapes=[pltpu.SemaphoreType.DMA(())],
        input_output_aliases={2: 0},
        compiler_params=pltpu.CompilerParams(dimension_semantics=("arbitrary",),
                                             has_side_effects=True, vmem_limit_bytes=VMEM_LIMIT),
        name="dispatch",
    )(dest, h2, xs0)


def _expert_kernel(be_ref, nu_ref, x_ref, g_ref, w1_ref, w3_ref, w2_ref, y_ref):
    del be_ref

    @pl.when(pl.program_id(0) < nu_ref[0])
    def _():
        xb = _rms(x_ref[...], g_ref[...]).astype(BF16)
        a = _dot(xb, w1_ref[...])
        hb = (a * jax.nn.sigmoid(a)) * _dot(xb, w3_ref[...])
        y_ref[...] = _dot(hb.astype(BF16), w2_ref[...])

    @pl.when(pl.program_id(0) >= nu_ref[0])
    def _():
        y_ref[...] = jnp.zeros(y_ref.shape, y_ref.dtype)


def _experts(block_expert, n_used, xs, gain, w1, w3, w2, tb):
    n_slots, d = xs.shape
    de = w1.shape[-1]
    nb = n_slots // tb
    row = lambda i, be, nu: (i, 0)
    grid_spec = pltpu.PrefetchScalarGridSpec(
        num_scalar_prefetch=2,
        grid=(nb,),
        in_specs=[pl.BlockSpec((tb, d), row),
                  pl.BlockSpec((1, d), lambda i, be, nu: (0, 0)),
                  pl.BlockSpec((None, d, de), lambda i, be, nu: (be[i], 0, 0)),
                  pl.BlockSpec((None, d, de), lambda i, be, nu: (be[i], 0, 0)),
                  pl.BlockSpec((None, de, d), lambda i, be, nu: (be[i], 0, 0))],
        out_specs=pl.BlockSpec((tb, d), row),
    )
    return pl.pallas_call(
        _expert_kernel,
        grid_spec=grid_spec,
        out_shape=jax.ShapeDtypeStruct((n_slots, d), F32),
        compiler_params=pltpu.CompilerParams(dimension_semantics=("arbitrary",),
                                             vmem_limit_bytes=VMEM_LIMIT),
        name="experts",
    )(block_expert, n_used, xs, gain, w1, w3, w2)


def _combine_kernel(dest_ref, mf_ref, h2_ref, g_ref, y_hbm, o_ref, ybuf, sem, *, tm):
    def row_copy(src_row, j, t):
        return pltpu.make_async_copy(y_hbm.at[pl.ds(src_row, 1)], ybuf.at[j, pl.ds(t, 1)], sem)

    def issue(t, carry):
        for j in range(2):
            row_copy(dest_ref[j, t], j, t).start()
        return carry

    def drain(t, carry):
        for j in range(2):
            row_copy(0, j, 0).wait()
        return carry

    lax.fori_loop(0, tm, issue, 0, unroll=ROW_DMA_UNROLL)
    lax.fori_loop(0, tm, drain, 0, unroll=ROW_DMA_UNROLL)

    gates = jnp.concatenate([mf_ref[...], jnp.zeros((LANES - 8, tm), F32)], axis=0)
    gt = jnp.transpose(gates)
    h = h2_ref[...] + gt[:, 0:1] * ybuf[0] + gt[:, 1:2] * ybuf[1]
    o_ref[...] = _rms(h, g_ref[...])


def _combine(dest, mf, h2, gain, y, tm):
    n, d = h2.shape
    kern = functools.partial(_combine_kernel, tm=tm)
    return pl.pallas_call(
        kern,
        grid=(n // tm,),
        in_specs=[pl.BlockSpec((8, tm), lambda i: (0, i), memory_space=pltpu.SMEM),
                  pl.BlockSpec((8, tm), lambda i: (0, i)),
                  pl.BlockSpec((tm, d), lambda i: (i, 0)),
                  pl.BlockSpec((1, d), lambda i: (0, 0)),
                  pl.BlockSpec(memory_space=pl.ANY)],
        out_specs=pl.BlockSpec((tm, d), lambda i: (i, 0)),
        out_shape=jax.ShapeDtypeStruct((n, d), F32),
        scratch_shapes=[pltpu.VMEM((2, tm, d), F32), pltpu.SemaphoreType.DMA(())],
        compiler_params=pltpu.CompilerParams(dimension_semantics=("arbitrary",),
                                             vmem_limit_bytes=VMEM_LIMIT),
        name="combine",
    )(dest, mf, h2, gain, y)


def _sigma_parts(n_heads):
    sigma = jnp.asarray([2.0 ** (-8.0 * (i + 1) / n_heads) for i in range(n_heads)], dtype=F32) * LOG2E
    s1 = sigma.astype(BF16).astype(F32)
    s2 = (sigma - s1).astype(BF16).astype(F32)
    s3 = ((sigma - s1) - s2).astype(BF16).astype(F32)
    return jnp.stack([s1, s2, s3], axis=1).reshape(-1)


def _prep_w_in(w_in):
    splits = []
    acc = 0
    for w in IN_WIDTHS[:-1]:
        acc += w
        splits.append(acc)
    qa, ka, va, qb, kb, vb, qi, ki, wi = jnp.split(w_in, splits, axis=1)
    w_n = jnp.concatenate([qa * (DH_A ** -0.5 * LOG2E), ka, qb * (DH_B ** -0.5 * LOG2E), kb, kb, qi, ki, ki], axis=1)
    pad = jnp.zeros((w_in.shape[0], WIT_ROWS - IN_WIDTHS[-1]), w_in.dtype)
    w_t = jnp.concatenate([va, vb, wi * (H_I ** -0.5 * D_I ** -0.5), pad], axis=1).T
    return w_n.astype(BF16), w_t.astype(BF16)


def _prep_router(w_group, w_router):
    d = w_group.shape[0]
    w = jnp.concatenate([w_router, w_group, jnp.zeros((d, LANES - N_EXPERTS - N_GROUPS), F32)], axis=1).T
    hi = w.astype(BF16)
    lo = (w - hi.astype(F32)).astype(BF16)
    return hi, lo


def _block_sizes(s):
    return dict(tm_proj=512, tq=256, tk=TK, tm_merge=min(512, s), tm_route=512, tm_disp=512, tb=512, tm_comb=256)


def kernel(x, mem, mix_norm, w_in, lam_q1, lam_k1, lam_q2, lam_k2, diff_subln, w_branch_a, w_branch_b, w_gate,
           b_gate, w_out, cross_norm, mem_norm, w_cq, w_ckv, w_co, ffn_norm, w_group, b_group, w_router,
           b_router, w1, w3, w2, final_norm):
    b, s, d = x.shape
    n = b * s
    m = mem.shape[1]
    bs = _block_sizes(s)
    k_sel = min(TOPK_MAX, s // 4)
    row = lambda v: v.reshape(1, -1).astype(F32)

    w_n, w_t = _prep_w_in(w_in[0])
    qa, ka, qb, kb2, qi, kik, vat, vbt, wit = _proj(x.reshape(n, d), row(mix_norm[0]), w_n, w_t, bs["tm_proj"])
    qa, ka, qb, kb2, qi, kik = [o.reshape(b, s, -1) for o in (qa, ka, qb, kb2, qi, kik)]
    vat = vat.reshape(b, s // TK, -1, TK)
    vbt = vbt.reshape(b, s // TK, -1, TK)
    lam = (jnp.exp(jnp.sum(lam_q1[0].astype(F32) * lam_k1[0].astype(F32)))
           - jnp.exp(jnp.sum(lam_q2[0].astype(F32) * lam_k2[0].astype(F32))) + LAMBDA_INIT).reshape(1)
    subln_t = jnp.broadcast_to(diff_subln[0].astype(F32)[:, None], (LANES, LANES))
    ya = _diff_attention(qa, ka, vat, _sigma_parts(H_A), lam, subln_t, bs["tq"], bs["tk"])
    yb = _dsa_attention(qb, kb2, vbt, qi, kik, wit, _sigma_parts(H_B), k_sel, bs["tq"], bs["tk"])

    mkv = _memkv(mem.reshape(b * m, d), row(mem_norm[0]), w_ckv[0].astype(BF16), min(512, b * m))
    w_r_hi, w_r_lo = _prep_router(w_group[0], w_router[0])
    params = dict(mix_g=row(mix_norm[0]), w_gate=w_gate[0].astype(BF16), b_gate=row(b_gate[0]),
                  w_a=w_branch_a[0].astype(BF16), w_b=w_branch_b[0].astype(BF16), w_out=w_out[0].astype(BF16),
                  cross_g=row(cross_norm[0]), w_cq=(w_cq[0] * (d // H_X) ** -0.5).astype(BF16),
                  w_co=w_co[0].astype(BF16), ffn_g=row(ffn_norm[0]), w_r_hi=w_r_hi, w_r_lo=w_r_lo)
    h2, lg_t = _merge(x, ya, yb, mkv.reshape(b, m, 2 * d), params, bs["tm_merge"])
    h2 = h2.reshape(n, d)

    bias = jnp.concatenate([b_router[0], b_group[0], jnp.zeros((LANES - N_EXPERTS - N_GROUPS,), F32)])
    mi, mf, cnt = _route(lg_t, bias.reshape(LANES, 1).astype(F32), bs["tm_route"])
    tb = bs["tb"]
    counts = cnt[:, 0].astype(I32)
    padded = (counts + tb - 1) // tb * tb
    pad_end = jnp.cumsum(padded)
    pad_start = (pad_end - padded).astype(I32)
    n_blocks = (2 * n) // tb + N_EXPERTS
    blk_first = jnp.arange(n_blocks, dtype=I32) * tb
    block_expert = jnp.minimum(jnp.sum((pad_end[None, :] <= blk_first[:, None]).astype(I32), axis=1), N_EXPERTS - 1)
    n_used = (pad_end[-1] // tb).astype(I32).reshape(1)
    start_of = jnp.sum(jnp.where(mi[0:2][None] == jnp.arange(N_EXPERTS, dtype=I32)[:, None, None],
                                 pad_start[:, None, None], 0), axis=0)
    dest = jnp.concatenate([start_of + mi[2:4], jnp.zeros((6, n), I32)], axis=0)

    xs = _dispatch(dest, h2, n_blocks * tb, bs["tm_disp"])
    y = _experts(block_expert, n_used, xs, row(ffn_norm[0]), w1[0].astype(BF16), w3[0].astype(BF16),
                 w2[0].astype(BF16), tb)
    out = _combine(dest, mf, h2, row(final_norm), y, bs["tm_comb"])
    return out.reshape(b, s, d)
```

```python
import functools

import jax
import jax.numpy as jnp
from jax import lax
from jax.experimental import pallas as pl
from jax.experimental.pallas import tpu as pltpu

F32 = jnp.float32
BF16 = jnp.bfloat16
I32 = jnp.int32

EPS = 1e-6
CHUNK_SHIFT = 6
H_A, DH_A = 4, 64
H_B, DH_B = 8, 64
H_I, D_I = 4, 64
TOPK_MAX = 256
H_X = 4
N_GROUPS, EXP_PER_GROUP = 4, 8
N_EXPERTS = N_GROUPS * EXP_PER_GROUP
LAMBDA_INIT = 0.8 - 0.6 * 1.0
LANES = 128
NEG = -1e30
INT_MIN = -2147483648
KEY_NEG_INF = -2139095041
VMEM_LIMIT = 56 * 1024 * 1024

LOG2E = 1.4426950408889634
TK = 256
WIT_ROWS = 16
IN_WIDTHS = (512, 512, 512, 512, 64, 64, 256, 64, 4)


def _rms(x, g):
    ms = jnp.mean(x * x, axis=-1, keepdims=True)
    return (x * lax.rsqrt(ms + EPS)) * g


def _dot_nt(a, b):
    return lax.dot_general(a, b, (((1,), (1,)), ((), ())), preferred_element_type=F32)


def _dot(a, b):
    return jnp.dot(a, b, preferred_element_type=F32)


def _proj_kernel(x_ref, g_ref, wn_ref, wt_ref, qa_ref, ka_ref, qb_ref, kb_ref, qi_ref, kik_ref,
                 vat_ref, vbt_ref, wit_ref):
    xn = _rms(x_ref[...], g_ref[...]).astype(BF16)
    col = 0
    for ref in (qa_ref, ka_ref, qb_ref, kb_ref, qi_ref, kik_ref):
        n = ref.shape[-1]
        ref[...] = _dot(xn, wn_ref[:, col:col + n]).astype(ref.dtype)
        col += n
    row = 0
    for ref in (vat_ref, vbt_ref):
        r = ref.shape[1]
        yt = _dot_nt(wt_ref[row:row + r, :], xn)
        for c in range(ref.shape[0]):
            ref[c] = yt[:, c * TK:(c + 1) * TK].astype(ref.dtype)
        row += r
    wit_ref[...] = _dot_nt(wt_ref[row:row + wit_ref.shape[0], :], xn)


def _proj(x2, gain, w_n, w_t, tm):
    n, d = x2.shape
    kb = tm // TK
    tok = lambda w: pl.BlockSpec((tm, w), lambda i: (i, 0))
    widths = (H_A * 2 * DH_A, H_A * 2 * DH_A, H_B * DH_B, LANES, H_I * D_I, LANES)
    return pl.pallas_call(
        _proj_kernel,
        grid=(n // tm,),
        in_specs=[pl.BlockSpec((tm, d), lambda i: (i, 0)),
                  pl.BlockSpec((1, d), lambda i: (0, 0)),
                  pl.BlockSpec(w_n.shape, lambda i: (0, 0)),
                  pl.BlockSpec(w_t.shape, lambda i: (0, 0))],
        out_specs=[tok(w) for w in widths] + [
            pl.BlockSpec((kb, H_A * 2 * DH_A, TK), lambda i: (i, 0, 0)),
            pl.BlockSpec((kb, DH_B, TK), lambda i: (i, 0, 0)),
            pl.BlockSpec((WIT_ROWS, tm), lambda i: (0, i))],
        out_shape=[jax.ShapeDtypeStruct((n, w), BF16) for w in widths] + [
            jax.ShapeDtypeStruct((n // TK, H_A * 2 * DH_A, TK), BF16),
            jax.ShapeDtypeStruct((n // TK, DH_B, TK), BF16),
            jax.ShapeDtypeStruct((WIT_ROWS, n), F32)],
        compiler_params=pltpu.CompilerParams(dimension_semantics=("arbitrary",),
                                             vmem_limit_bytes=VMEM_LIMIT),
        name="proj",
    )(x2, gain, w_n, w_t)


def _aug_query(sig_ref, h, tq):
    lane = lax.broadcasted_iota(I32, (tq, LANES), 1)
    i = lax.broadcasted_iota(I32, (tq, LANES), 0).astype(F32)
    s1, s2, s3 = sig_ref[3 * h], sig_ref[3 * h + 1], sig_ref[3 * h + 2]
    c = ((s1 + s2) + s3) * i
    c1 = c.astype(BF16).astype(F32)
    c2 = (c - c1).astype(BF16).astype(F32)
    c3 = (c - c1) - c2
    out = jnp.zeros((tq, LANES), F32)
    for n, v in enumerate((c1, c2, c3, s1, s2, s3, s1, s2, s3, -s1, -s2, -s3)):
        out = jnp.where(lane == n, v, out)
    return out.astype(BF16)


def _aug_key(tk, off):
    lane = lax.broadcasted_iota(I32, (tk, LANES), 1)
    j = lax.broadcasted_iota(I32, (tk, LANES), 0)
    j_lo = (j & 255).astype(F32)
    j_hi = (j - (j & 255)).astype(F32)
    base = jnp.where(lane < 3, -1.0, jnp.where(lane < 6, j_lo, jnp.where(lane < 9, j_hi, jnp.where(lane < 12, off, 0.0))))
    return base.astype(BF16)


def _sigma(sig_ref, h):
    return (sig_ref[3 * h] + sig_ref[3 * h + 1]) + sig_ref[3 * h + 2]


def _softmax_step(r, st, vt, m_sc, l_sc, acc_sc):
    m_prev = m_sc[r][0:1, :]
    m_next = jnp.maximum(m_prev, jnp.max(st, axis=0, keepdims=True))
    p = jnp.exp2(st - m_next)
    alpha = jnp.exp2(m_prev - m_next)
    l_next = alpha * l_sc[r][0:1, :] + jnp.sum(p, axis=0, keepdims=True)
    acc_sc[r] = alpha * acc_sc[r] + _dot(vt, p.astype(BF16))
    m_sc[r] = jnp.broadcast_to(m_next, m_sc.shape[1:])
    l_sc[r] = jnp.broadcast_to(l_next, l_sc.shape[1:])


def _diff_kernel(sig_ref, lam_ref, q_ref, k_ref, vt_ref, g_ref, o_ref, qaug_sc, s_sc, m_sc, l_sc, acc_sc, *, tq, tk):
    qblk = pl.program_id(1)
    lam = lam_ref[0]
    lane_k = lax.broadcasted_iota(I32, (tk, LANES), 1)
    for h in range(H_A):
        qaug_sc[h] = jnp.concatenate([q_ref[:, h * LANES:(h + 1) * LANES], _aug_query(sig_ref, h, tq)], axis=1)
    m_sc[...] = jnp.full(m_sc.shape, NEG, F32)
    l_sc[...] = jnp.zeros(l_sc.shape, F32)
    acc_sc[...] = jnp.zeros(acc_sc.shape, F32)

    def block(kb, diag):
        off = pl.multiple_of(kb * tk, tk)
        k = k_ref[pl.ds(off, tk), :]
        rel = qblk * tq - kb * tk
        augk = _aug_key(tk, rel.astype(F32))
        if diag:
            jj = lax.broadcasted_iota(I32, (tk, tq), 0)
            ii = lax.broadcasted_iota(I32, (tk, tq), 1)
            allowed = ((off + jj) >> CHUNK_SHIFT) <= ((qblk * tq + ii) >> CHUNK_SHIFT)
            ahead = jnp.maximum(jj - ii - rel, 0).astype(F32)
        for h in range(H_A):
            kh = k[:, h * LANES:(h + 1) * LANES]
            zero = jnp.zeros_like(kh)
            qa = qaug_sc[h]
            for m in range(2):
                km = jnp.where(lane_k < DH_A if m == 0 else lane_k >= DH_A, kh, zero)
                st = _dot_nt(jnp.concatenate([km, augk], axis=1), qa)
                if diag:
                    st = jnp.where(allowed, st - (2.0 * _sigma(sig_ref, h)) * ahead, NEG)
                s_sc[2 * h + m] = st
        for h in range(H_A):
            vt = vt_ref[kb, h * LANES:(h + 1) * LANES, :]
            for m in range(2):
                _softmax_step(2 * h + m, s_sc[2 * h + m], vt, m_sc, l_sc, acc_sc)

    def off_diagonal(kb, carry):
        block(kb, False)
        return carry

    n_before = (qblk * tq) // tk
    lax.fori_loop(0, n_before, off_diagonal, 0)
    for c in range(max(1, tq // tk)):
        block(n_before + c, True)

    gain = jnp.concatenate([g_ref[...]] * (tq // LANES), axis=1)
    for h in range(H_A):
        y = acc_sc[2 * h] / l_sc[2 * h][0:1, :] - lam * (acc_sc[2 * h + 1] / l_sc[2 * h + 1][0:1, :])
        ms = jnp.mean(y * y, axis=0, keepdims=True)
        yn = (y * lax.rsqrt(ms + EPS)) * gain
        o_ref[:, h * LANES:(h + 1) * LANES] = (jnp.transpose(yn) * (1.0 - LAMBDA_INIT)).astype(o_ref.dtype)


def _diff_attention(qa, ka, vat, sig, lam, subln_t, tq, tk):
    b, s, w = qa.shape
    assert tq % tk == 0 or tk % tq == 0
    kern = functools.partial(_diff_kernel, tq=tq, tk=tk)
    smem = pl.BlockSpec(memory_space=pltpu.SMEM)
    return pl.pallas_call(
        kern,
        grid=(b, s // tq),
        in_specs=[smem, smem,
                  pl.BlockSpec((None, tq, w), lambda bi, i: (bi, i, 0)),
                  pl.BlockSpec((None, s, w), lambda bi, i: (bi, 0, 0)),
                  pl.BlockSpec((None, s // tk, w, tk), lambda bi, i: (bi, 0, 0, 0)),
                  pl.BlockSpec((LANES, LANES), lambda bi, i: (0, 0))],
        out_specs=pl.BlockSpec((None, tq, w), lambda bi, i: (bi, i, 0)),
        out_shape=jax.ShapeDtypeStruct((b, s, w), BF16),
        scratch_shapes=[pltpu.VMEM((H_A, tq, 2 * LANES), BF16),
                        pltpu.VMEM((2 * H_A, tk, tq), F32),
                        pltpu.VMEM((2 * H_A, 8, tq), F32),
                        pltpu.VMEM((2 * H_A, 8, tq), F32),
                        pltpu.VMEM((2 * H_A, LANES, tq), F32)],
        compiler_params=pltpu.CompilerParams(dimension_semantics=("arbitrary",) * 2,
                                             vmem_limit_bytes=VMEM_LIMIT),
        name="diff_attn",
    )(sig, lam, qa, ka, vat, subln_t)


def _dsa_kernel(sig_ref, qb_ref, kb_ref, vt_ref, qi_ref, kik_ref, wit_ref, o_ref,
                qaug_sc, key_sc, s_sc, m_sc, l_sc, acc_sc, cut_sc, *, tq, tk, k_sel, idx_bits):
    qblk = pl.program_id(1)
    n_before = (qblk * tq) // tk
    nvis = n_before + max(1, tq // tk)
    lane_k = lax.broadcasted_iota(I32, (tk, LANES), 1)
    jj = lax.broadcasted_iota(I32, (tk, tq), 0)
    ii = lax.broadcasted_iota(I32, (tk, tq), 1)
    t_chunk = (qblk * tq + ii) >> CHUNK_SHIFT
    w_idx = wit_ref[...]
    qidx = qi_ref[...]

    def index_block(kb, carry):
        off = pl.multiple_of(kb * tk, tk)
        kik = kik_ref[pl.ds(off, tk), :]
        zero = jnp.zeros_like(kik)
        k_half = (jnp.where(lane_k < D_I, kik, zero), jnp.where(lane_k >= D_I, kik, zero))
        isc = jnp.zeros((tk, tq), F32)
        for h in range(H_I):
            d = _dot_nt(k_half[h % 2], qidx[:, (h // 2) * LANES:(h // 2 + 1) * LANES])
            isc = isc + w_idx[h:h + 1, :] * jnp.maximum(d, 0.0)
        bits = lax.bitcast_convert_type(isc, I32)
        key = jnp.where(bits < 0, bits ^ 0x7FFFFFFF, bits)
        key = jnp.where(isc == 0.0, 0, key)
        allowed = ((off + jj) >> CHUNK_SHIFT) <= t_chunk
        key_sc[kb] = jnp.where(allowed, key, KEY_NEG_INF)
        return carry

    lax.fori_loop(0, nvis, index_block, 0)

    def count_ge(cand):
        def body(kb, acc):
            hit = jnp.where(key_sc[kb] >= cand, 1.0, 0.0)
            return acc + jnp.sum(hit.reshape(tk // 8, 8, tq), axis=0)
        acc = lax.fori_loop(0, nvis, body, jnp.zeros((8, tq), F32))
        return jnp.sum(acc, axis=0, keepdims=True)

    kf = float(k_sel)
    zero_i = jnp.zeros((1, tq), I32)
    thr = jnp.where(count_ge(zero_i) >= kf, zero_i, jnp.full((1, tq), INT_MIN, I32))

    def bit_step(i, t):
        cand = t + lax.shift_left(jnp.int32(1), jnp.asarray(30 - i, I32))
        return jnp.where(count_ge(cand) >= kf, cand, t)

    thr = lax.fori_loop(0, 31, bit_step, thr)

    need = kf - count_ge(thr + 1)
    surplus = jnp.where(thr == KEY_NEG_INF, 0.0, count_ge(thr) - kf)
    thr_eq = jnp.where(thr == KEY_NEG_INF, INT_MIN, thr)
    cut_sc[...] = jnp.full(cut_sc.shape, (1 << idx_bits) - 1, I32)

    @pl.when(jnp.max(surplus) > 0.0)
    def _():
        def count_eq_before(cand):
            def body(kb, acc):
                hit = jnp.where(kb * tk + jj < cand, 1.0, 0.0)
                return acc + jnp.sum(jnp.where(key_sc[kb] == thr_eq, hit, 0.0), axis=0, keepdims=True)
            return lax.fori_loop(0, nvis, body, jnp.zeros((1, tq), F32))

        def idx_step(i, p):
            cand = p + lax.shift_left(jnp.int32(1), jnp.asarray(idx_bits - 1 - i, I32))
            return jnp.where(count_eq_before(cand) < need, cand, p)

        cut_sc[0:1, :] = lax.fori_loop(0, idx_bits, idx_step, zero_i)

    last_eq = cut_sc[0:1, :]

    for h in range(H_B):
        qaug_sc[h] = jnp.concatenate([qb_ref[:, (h // 2) * LANES:(h // 2 + 1) * LANES],
                                      _aug_query(sig_ref, h, tq)], axis=1)
    m_sc[...] = jnp.full(m_sc.shape, NEG, F32)
    l_sc[...] = jnp.zeros(l_sc.shape, F32)
    acc_sc[...] = jnp.zeros(acc_sc.shape, F32)

    def attend(kb, diag):
        off = pl.multiple_of(kb * tk, tk)
        kk = kb_ref[pl.ds(off, tk), :]
        zero = jnp.zeros_like(kk)
        rel = qblk * tq - kb * tk
        augk = _aug_key(tk, rel.astype(F32))
        k_aug = (jnp.concatenate([jnp.where(lane_k < DH_B, kk, zero), augk], axis=1),
                 jnp.concatenate([jnp.where(lane_k >= DH_B, kk, zero), augk], axis=1))
        vt = vt_ref[kb]
        key = key_sc[kb]
        keep_tie = jnp.where(off + jj <= last_eq, 0.0, NEG)
        mask = jnp.where(key > thr, 0.0, jnp.where(key == thr_eq, keep_tie, NEG))
        if diag:
            ahead = jnp.maximum(jj - ii - rel, 0).astype(F32)
        for h in range(H_B):
            st = _dot_nt(k_aug[h % 2], qaug_sc[h]) + mask
            if diag:
                st = st - (2.0 * _sigma(sig_ref, h)) * ahead
            s_sc[h] = st
        for h in range(H_B):
            _softmax_step(h, s_sc[h], vt, m_sc, l_sc, acc_sc)

    def off_diagonal(kb, carry):
        attend(kb, False)
        return carry

    lax.fori_loop(0, n_before, off_diagonal, 0)
    for c in range(max(1, tq // tk)):
        attend(n_before + c, True)

    yt = jnp.concatenate([acc_sc[h] / l_sc[h][0:1, :] for h in range(H_B)], axis=0)
    o_ref[...] = jnp.transpose(yt).astype(o_ref.dtype)


def _dsa_attention(qb, kb2, vbt, qi, kik, wit, sig, k_sel, tq, tk):
    b, s, w = qb.shape
    assert tq % tk == 0 or tk % tq == 0
    idx_bits = max(1, (s - 1).bit_length())
    kern = functools.partial(_dsa_kernel, tq=tq, tk=tk, k_sel=k_sel, idx_bits=idx_bits)
    tile = lambda c: pl.BlockSpec((None, tq, c), lambda bi, i: (bi, i, 0))
    full = lambda c: pl.BlockSpec((None, s, c), lambda bi, i: (bi, 0, 0))
    nq = s // tq
    return pl.pallas_call(
        kern,
        grid=(b, nq),
        in_specs=[pl.BlockSpec(memory_space=pltpu.SMEM),
                  tile(w), full(LANES),
                  pl.BlockSpec((None, s // tk, DH_B, tk), lambda bi, i: (bi, 0, 0, 0)),
                  tile(H_I * D_I), full(LANES),
                  pl.BlockSpec((WIT_ROWS, tq), lambda bi, i: (0, bi * nq + i))],
        out_specs=tile(w),
        out_shape=jax.ShapeDtypeStruct((b, s, w), BF16),
        scratch_shapes=[pltpu.VMEM((H_B, tq, 2 * LANES), BF16),
                        pltpu.VMEM((s // tk, tk, tq), I32),
                        pltpu.VMEM((H_B, tk, tq), F32),
                        pltpu.VMEM((H_B, 8, tq), F32),
                        pltpu.VMEM((H_B, 8, tq), F32),
                        pltpu.VMEM((H_B, DH_B, tq), F32),
                        pltpu.VMEM((8, tq), I32)],
        compiler_params=pltpu.CompilerParams(dimension_semantics=("arbitrary",) * 2,
                                             vmem_limit_bytes=VMEM_LIMIT),
        name="dsa_attn",
    )(sig, qb, kb2, vbt, qi, kik, wit)


def _memkv_kernel(m_ref, g_ref, w_ref, o_ref):
    mn = _rms(m_ref[...], g_ref[...]).astype(BF16)
    o_ref[...] = _dot(mn, w_ref[...]).astype(o_ref.dtype)


def _memkv(mem2, gain, w_ckv, tm):
    n, d = mem2.shape
    return pl.pallas_call(
        _memkv_kernel,
        grid=(n // tm,),
        in_specs=[pl.BlockSpec((tm, d), lambda i: (i, 0)),
                  pl.BlockSpec((1, d), lambda i: (0, 0)),
                  pl.BlockSpec(w_ckv.shape, lambda i: (0, 0))],
        out_specs=pl.BlockSpec((tm, w_ckv.shape[1]), lambda i: (i, 0)),
        out_shape=jax.ShapeDtypeStruct((n, w_ckv.shape[1]), BF16),
        compiler_params=pltpu.CompilerParams(dimension_semantics=("arbitrary",),
                                             vmem_limit_bytes=VMEM_LIMIT),
        name="memkv",
    )(mem2, gain, w_ckv)


def _merge_kernel(x_ref, ya_ref, yb_ref, mkv_ref, mixg_ref, wgate_ref, bgate_ref, wa_ref, wb_ref, wout_ref,
                  crossg_ref, wcq_ref, wco_ref, ffng_ref, wrh_ref, wrl_ref, h2_ref, lg_ref):
    d = x_ref.shape[-1]
    dh = d // H_X
    x = x_ref[...]
    xn = _rms(x, mixg_ref[...]).astype(BF16)
    gates = jax.nn.sigmoid(_dot(xn, wgate_ref[...]) + bgate_ref[...])
    merged = gates[:, :d] * _dot(ya_ref[...], wa_ref[...]) + gates[:, d:] * _dot(yb_ref[...], wb_ref[...])
    h1 = x + _dot(merged.astype(BF16), wout_ref[...])

    q = _dot(_rms(h1, crossg_ref[...]).astype(BF16), wcq_ref[...]).astype(BF16)
    heads = []
    for h in range(H_X):
        k = mkv_ref[:, h * dh:(h + 1) * dh]
        v = mkv_ref[:, d + h * dh:d + (h + 1) * dh]
        s = _dot_nt(q[:, h * dh:(h + 1) * dh], k)
        e = jnp.exp(s - jnp.max(s, axis=1, keepdims=True))
        p = e / jnp.sum(e, axis=1, keepdims=True)
        heads.append(_dot(p.astype(BF16), v).astype(BF16))
    h2 = h1 + _dot(jnp.concatenate(heads, axis=1), wco_ref[...])
    h2_ref[...] = h2

    f = _rms(h2, ffng_ref[...])
    f_hi = f.astype(BF16)
    f_lo = (f - f_hi.astype(F32)).astype(BF16)
    w_hi = wrh_ref[...]
    lg_ref[...] = _dot_nt(w_hi, f_hi) + _dot_nt(w_hi, f_lo) + _dot_nt(wrl_ref[...], f_hi)


def _merge(x3, ya, yb, mkv, p, tm):
    b, s, d = x3.shape
    m = mkv.shape[1]
    tok = lambda w: pl.BlockSpec((None, tm, w), lambda bi, i: (bi, i, 0))
    const = lambda a: pl.BlockSpec(a.shape, lambda bi, i: (0,) * a.ndim, pipeline_mode=pl.Buffered(1))
    consts = (p["mix_g"], p["w_gate"], p["b_gate"], p["w_a"], p["w_b"], p["w_out"], p["cross_g"],
              p["w_cq"], p["w_co"], p["ffn_g"], p["w_r_hi"], p["w_r_lo"])
    return pl.pallas_call(
        _merge_kernel,
        grid=(b, s // tm),
        in_specs=[tok(d), tok(ya.shape[-1]), tok(yb.shape[-1]),
                  pl.BlockSpec((None, m, 2 * d), lambda bi, i: (bi, 0, 0))] + [const(a) for a in consts],
        out_specs=[tok(d), pl.BlockSpec((LANES, tm), lambda bi, i: (0, bi * (s // tm) + i))],
        out_shape=[jax.ShapeDtypeStruct((b, s, d), F32), jax.ShapeDtypeStruct((LANES, b * s), F32)],
        compiler_params=pltpu.CompilerParams(dimension_semantics=("arbitrary",) * 2,
                                             vmem_limit_bytes=VMEM_LIMIT),
        name="merge",
    )(x3, ya, yb, mkv, *consts)


def _route_kernel(lg_ref, bias_ref, mi_ref, mf_ref, cnt_ref, carry_sc, *, tm):
    @pl.when(pl.program_id(0) == 0)
    def _():
        carry_sc[...] = jnp.zeros(carry_sc.shape, F32)

    lg = lg_ref[...] + bias_ref[...]
    e = lg[0:N_EXPERTS]
    g = lg[N_EXPERTS:N_EXPERTS + 8]
    row_g = lax.broadcasted_iota(I32, (8, tm), 0)
    g = jnp.where(row_g < N_GROUPS, g, -jnp.inf)
    gmax = jnp.max(g, axis=0, keepdims=True)
    g_idx = jnp.min(jnp.where(g == gmax, row_g, N_GROUPS), axis=0, keepdims=True)
    p_g = 1.0 / jnp.sum(jnp.exp(g - gmax), axis=0, keepdims=True)

    row_e = lax.broadcasted_iota(I32, (N_EXPERTS, tm), 0)
    in_grp = (row_e >> 3) == g_idx
    emax = jnp.max(jnp.where(in_grp, e, -jnp.inf), axis=0, keepdims=True)
    ex = jnp.where(in_grp, jnp.exp(jnp.where(in_grp, e - emax, 0.0)), 0.0)
    probs = ex / jnp.sum(ex, axis=0, keepdims=True)
    big = N_EXPERTS
    p1 = jnp.max(probs, axis=0, keepdims=True)
    i1 = jnp.min(jnp.where(in_grp, jnp.where(probs == p1, row_e, big), big), axis=0, keepdims=True)
    probs2 = jnp.where(in_grp, jnp.where(row_e == i1, -1.0, probs), -1.0)
    p2 = jnp.max(probs2, axis=0, keepdims=True)
    i2 = jnp.min(jnp.where(probs2 == p2, row_e, big), axis=0, keepdims=True)
    denom = p1 + p2
    gate1 = p_g * p1 / denom
    gate2 = p_g * p2 / denom

    used = jnp.where(row_e == i1, 1.0, jnp.where(row_e == i2, 1.0, 0.0))
    r = lax.broadcasted_iota(I32, (tm, tm), 0)
    c = lax.broadcasted_iota(I32, (tm, tm), 1)
    before = jnp.where(r < c, 1.0, 0.0).astype(BF16)
    excl = _dot(used.astype(BF16), before) + carry_sc[:, 0:1]
    rank1 = jnp.sum(jnp.where(row_e == i1, excl, 0.0), axis=0, keepdims=True)
    rank2 = jnp.sum(jnp.where(row_e == i2, excl, 0.0), axis=0, keepdims=True)
    carry_sc[...] = carry_sc[...] + jnp.sum(used, axis=1, keepdims=True)
    cnt_ref[...] = carry_sc[...]

    mi_ref[...] = jnp.zeros(mi_ref.shape, I32)
    mi_ref[0:1, :] = i1
    mi_ref[1:2, :] = i2
    mi_ref[2:3, :] = rank1.astype(I32)
    mi_ref[3:4, :] = rank2.astype(I32)
    mf_ref[...] = jnp.zeros(mf_ref.shape, F32)
    mf_ref[0:1, :] = gate1
    mf_ref[1:2, :] = gate2


def _route(lg_t, bias, tm):
    n = lg_t.shape[1]
    kern = functools.partial(_route_kernel, tm=tm)
    return pl.pallas_call(
        kern,
        grid=(n // tm,),
        in_specs=[pl.BlockSpec((LANES, tm), lambda i: (0, i)),
                  pl.BlockSpec((LANES, 1), lambda i: (0, 0))],
        out_specs=[pl.BlockSpec((8, tm), lambda i: (0, i)),
                   pl.BlockSpec((8, tm), lambda i: (0, i)),
                   pl.BlockSpec((N_EXPERTS, LANES), lambda i: (0, 0))],
        out_shape=[jax.ShapeDtypeStruct((8, n), I32), jax.ShapeDtypeStruct((8, n), F32),
                   jax.ShapeDtypeStruct((N_EXPERTS, LANES), F32)],
        scratch_shapes=[pltpu.VMEM((N_EXPERTS, LANES), F32)],
        compiler_params=pltpu.CompilerParams(dimension_semantics=("arbitrary",),
                                             vmem_limit_bytes=VMEM_LIMIT),
        name="route",
    )(lg_t, bias)


ROW_DMA_UNROLL = 8


def _dispatch_kernel(dest_ref, h2_ref, xs_in_hbm, xs_hbm, sem, *, tm):
    del xs_in_hbm

    def row_copy(t0, u, dst_row):
        src = h2_ref.at[pl.ds(t0, ROW_DMA_UNROLL)].at[pl.ds(u, 1)]
        return pltpu.make_async_copy(src, xs_hbm.at[pl.ds(dst_row, 1)], sem)

    def issue(c, carry):
        t0 = pl.multiple_of(c * ROW_DMA_UNROLL, ROW_DMA_UNROLL)
        for u in range(ROW_DMA_UNROLL):
            for j in range(2):
                row_copy(t0, u, dest_ref[0, j * tm + t0 + u]).start()
        return carry

    def drain(t, carry):
        for j in range(2):
            row_copy(0, 0, 0).wait()
        return carry

    lax.fori_loop(0, tm // ROW_DMA_UNROLL, issue, 0)
    lax.fori_loop(0, tm, drain, 0, unroll=ROW_DMA_UNROLL)


def _dispatch(dest, h2, n_slots, tm):
    n, d = h2.shape
    kern = functools.partial(_dispatch_kernel, tm=tm)
    xs0 = jnp.zeros((n_slots, d), F32)
    return pl.pallas_call(
        kern,
        grid=(n // tm,),
        in_specs=[pl.BlockSpec((None, 1, 2 * tm), lambda i: (i, 0, 0), memory_space=pltpu.SMEM),
                  pl.BlockSpec((tm, d), lambda i: (i, 0)),
                  pl.BlockSpec(memory_space=pl.ANY)],
        out_specs=pl.BlockSpec(memory_space=pl.ANY),
        out_shape=jax.ShapeDtypeStruct((n_slots, d), F32),
        scratch_shapes=[pltpu.SemaphoreType.DMA(())],
        input_output_aliases={2: 0},
        compiler_params=pltpu.CompilerParams(dimension_semantics=("arbitrary",),
                                             has_side_effects=True, vmem_limit_bytes=VMEM_LIMIT),
        name="dispatch",
    )(dest, h2, xs0)


def _expert_kernel(be_ref, nu_ref, x_ref, g_ref, w1_ref, w3_ref, w2_ref, y_ref):
    del be_ref

    @pl.when(pl.program_id(0) < nu_ref[0])
    def _():
        xb = _rms(x_ref[...], g_ref[...]).astype(BF16)
        a = _dot(xb, w1_ref[...])
        hb = (a * jax.nn.sigmoid(a)) * _dot(xb, w3_ref[...])
        y_ref[...] = _dot(hb.astype(BF16), w2_ref[...])

    @pl.when(pl.program_id(0) >= nu_ref[0])
    def _():
        y_ref[...] = jnp.zeros(y_ref.shape, y_ref.dtype)


def _experts(block_expert, n_used, xs, gain, w1, w3, w2, tb):
    n_slots, d = xs.shape
    de = w1.shape[-1]
    nb = n_slots // tb
    row = lambda i, be, nu: (i, 0)
    grid_spec = pltpu.PrefetchScalarGridSpec(
        num_scalar_prefetch=2,
        grid=(nb,),
        in_specs=[pl.BlockSpec((tb, d), row),
                  pl.BlockSpec((1, d), lambda i, be, nu: (0, 0)),
                  pl.BlockSpec((None, d, de), lambda i, be, nu: (be[i], 0, 0)),
                  pl.BlockSpec((None, d, de), lambda i, be, nu: (be[i], 0, 0)),
                  pl.BlockSpec((None, de, d), lambda i, be, nu: (be[i], 0, 0))],
        out_specs=pl.BlockSpec((tb, d), row),
    )
    return pl.pallas_call(
        _expert_kernel,
        grid_spec=grid_spec,
        out_shape=jax.ShapeDtypeStruct((n_slots, d), F32),
        compiler_params=pltpu.CompilerParams(dimension_semantics=("arbitrary",),
                                             vmem_limit_bytes=VMEM_LIMIT),
        name="experts",
    )(block_expert, n_used, xs, gain, w1, w3, w2)


def _combine_kernel(dest_ref, mf_ref, h2_ref, g_ref, y_hbm, o_ref, ybuf, sem, *, tm):
    def row_copy(src_row, j, t0, u):
        dst = ybuf.at[j, pl.ds(t0, ROW_DMA_UNROLL)].at[pl.ds(u, 1)]
        return pltpu.make_async_copy(y_hbm.at[pl.ds(src_row, 1)], dst, sem)

    def issue(c, carry):
        t0 = pl.multiple_of(c * ROW_DMA_UNROLL, ROW_DMA_UNROLL)
        for u in range(ROW_DMA_UNROLL):
            for j in range(2):
                row_copy(dest_ref[0, j * tm + t0 + u], j, t0, u).start()
        return carry

    def drain(t, carry):
        for j in range(2):
            row_copy(0, j, 0, 0).wait()
        return carry

    lax.fori_loop(0, tm // ROW_DMA_UNROLL, issue, 0)
    lax.fori_loop(0, tm, drain, 0, unroll=ROW_DMA_UNROLL)

    gates = jnp.concatenate([mf_ref[...], jnp.zeros((LANES - 8, tm), F32)], axis=0)
    gt = jnp.transpose(gates)
    h = h2_ref[...] + gt[:, 0:1] * ybuf[0] + gt[:, 1:2] * ybuf[1]
    o_ref[...] = _rms(h, g_ref[...])


def _combine(dest, mf, h2, gain, y, tm):
    n, d = h2.shape
    kern = functools.partial(_combine_kernel, tm=tm)
    return pl.pallas_call(
        kern,
        grid=(n // tm,),
        in_specs=[pl.BlockSpec((None, 1, 2 * tm), lambda i: (i, 0, 0), memory_space=pltpu.SMEM),
                  pl.BlockSpec((8, tm), lambda i: (0, i)),
                  pl.BlockSpec((tm, d), lambda i: (i, 0)),
                  pl.BlockSpec((1, d), lambda i: (0, 0)),
                  pl.BlockSpec(memory_space=pl.ANY)],
        out_specs=pl.BlockSpec((tm, d), lambda i: (i, 0)),
        out_shape=jax.ShapeDtypeStruct((n, d), F32),
        scratch_shapes=[pltpu.VMEM((2, tm, d), F32), pltpu.SemaphoreType.DMA(())],
        compiler_params=pltpu.CompilerParams(dimension_semantics=("arbitrary",),
                                             vmem_limit_bytes=VMEM_LIMIT),
        name="combine",
    )(dest, mf, h2, gain, y)


def _sigma_parts(n_heads):
    sigma = jnp.asarray([2.0 ** (-8.0 * (i + 1) / n_heads) for i in range(n_heads)], dtype=F32) * LOG2E
    s1 = sigma.astype(BF16).astype(F32)
    s2 = (sigma - s1).astype(BF16).astype(F32)
    s3 = ((sigma - s1) - s2).astype(BF16).astype(F32)
    return jnp.stack([s1, s2, s3], axis=1).reshape(-1)


def _prep_w_in(w_in):
    splits = []
    acc = 0
    for w in IN_WIDTHS[:-1]:
        acc += w
        splits.append(acc)
    qa, ka, va, qb, kb, vb, qi, ki, wi = jnp.split(w_in, splits, axis=1)
    w_n = jnp.concatenate([qa * (DH_A ** -0.5 * LOG2E), ka, qb * (DH_B ** -0.5 * LOG2E), kb, kb, qi, ki, ki], axis=1)
    pad = jnp.zeros((w_in.shape[0], WIT_ROWS - IN_WIDTHS[-1]), w_in.dtype)
    w_t = jnp.concatenate([va, vb, wi * (H_I ** -0.5 * D_I ** -0.5), pad], axis=1).T
    return w_n.astype(BF16), w_t.astype(BF16)


def _prep_router(w_group, w_router):
    d = w_group.shape[0]
    w = jnp.concatenate([w_router, w_group, jnp.zeros((d, LANES - N_EXPERTS - N_GROUPS), F32)], axis=1).T
    hi = w.astype(BF16)
    lo = (w - hi.astype(F32)).astype(BF16)
    return hi, lo


def _block_sizes(s):
    return dict(tm_proj=512, tq=256, tk=TK, tm_merge=min(512, s), tm_route=512, tm_disp=512, tb=512, tm_comb=256)


def kernel(x, mem, mix_norm, w_in, lam_q1, lam_k1, lam_q2, lam_k2, diff_subln, w_branch_a, w_branch_b, w_gate,
           b_gate, w_out, cross_norm, mem_norm, w_cq, w_ckv, w_co, ffn_norm, w_group, b_group, w_router,
           b_router, w1, w3, w2, final_norm):
    b, s, d = x.shape
    n = b * s
    m = mem.shape[1]
    bs = _block_sizes(s)
    k_sel = min(TOPK_MAX, s // 4)
    row = lambda v: v.reshape(1, -1).astype(F32)

    w_n, w_t = _prep_w_in(w_in[0])
    qa, ka, qb, kb2, qi, kik, vat, vbt, wit = _proj(x.reshape(n, d), row(mix_norm[0]), w_n, w_t, bs["tm_proj"])
    qa, ka, qb, kb2, qi, kik = [o.reshape(b, s, -1) for o in (qa, ka, qb, kb2, qi, kik)]
    vat = vat.reshape(b, s // TK, -1, TK)
    vbt = vbt.reshape(b, s // TK, -1, TK)
    lam = (jnp.exp(jnp.sum(lam_q1[0].astype(F32) * lam_k1[0].astype(F32)))
           - jnp.exp(jnp.sum(lam_q2[0].astype(F32) * lam_k2[0].astype(F32))) + LAMBDA_INIT).reshape(1)
    subln_t = jnp.broadcast_to(diff_subln[0].astype(F32)[:, None], (LANES, LANES))
    ya = _diff_attention(qa, ka, vat, _sigma_parts(H_A), lam, subln_t, bs["tq"], bs["tk"])
    yb = _dsa_attention(qb, kb2, vbt, qi, kik, wit, _sigma_parts(H_B), k_sel, bs["tq"], bs["tk"])

    mkv = _memkv(mem.reshape(b * m, d), row(mem_norm[0]), w_ckv[0].astype(BF16), min(512, b * m))
    w_r_hi, w_r_lo = _prep_router(w_group[0], w_router[0])
    params = dict(mix_g=row(mix_norm[0]), w_gate=w_gate[0].astype(BF16), b_gate=row(b_gate[0]),
                  w_a=w_branch_a[0].astype(BF16), w_b=w_branch_b[0].astype(BF16), w_out=w_out[0].astype(BF16),
                  cross_g=row(cross_norm[0]), w_cq=(w_cq[0] * (d // H_X) ** -0.5).astype(BF16),
                  w_co=w_co[0].astype(BF16), ffn_g=row(ffn_norm[0]), w_r_hi=w_r_hi, w_r_lo=w_r_lo)
    h2, lg_t = _merge(x, ya, yb, mkv.reshape(b, m, 2 * d), params, bs["tm_merge"])
    h2 = h2.reshape(n, d)

    bias = jnp.concatenate([b_router[0], b_group[0], jnp.zeros((LANES - N_EXPERTS - N_GROUPS,), F32)])
    mi, mf, cnt = _route(lg_t, bias.reshape(LANES, 1).astype(F32), bs["tm_route"])
    tb = bs["tb"]
    counts = cnt[:, 0].astype(I32)
    padded = (counts + tb - 1) // tb * tb
    pad_end = jnp.cumsum(padded)
    pad_start = (pad_end - padded).astype(I32)
    n_blocks = (2 * n) // tb + N_EXPERTS
    blk_first = jnp.arange(n_blocks, dtype=I32) * tb
    block_expert = jnp.minimum(jnp.sum((pad_end[None, :] <= blk_first[:, None]).astype(I32), axis=1), N_EXPERTS - 1)
    n_used = (pad_end[-1] // tb).astype(I32).reshape(1)
    start_of = jnp.sum(jnp.where(mi[0:2][None] == jnp.arange(N_EXPERTS, dtype=I32)[:, None, None],
                                 pad_start[:, None, None], 0), axis=0)
    dest = start_of + mi[2:4]

    def per_tile(tm):
        return dest.reshape(2, n // tm, tm).transpose(1, 0, 2).reshape(n // tm, 1, 2 * tm)

    xs = _dispatch(per_tile(bs["tm_disp"]), h2, n_blocks * tb, bs["tm_disp"])
    y = _experts(block_expert, n_used, xs, row(ffn_norm[0]), w1[0].astype(BF16), w3[0].astype(BF16),
                 w2[0].astype(BF16), tb)
    out = _combine(per_tile(bs["tm_comb"]), mf, h2, row(final_norm), y, bs["tm_comb"])
    return out.reshape(b, s, d)
```

```python
import functools

import jax
import jax.numpy as jnp
from jax import lax
from jax.experimental import pallas as pl
from jax.experimental.pallas import tpu as pltpu

F32 = jnp.float32
BF16 = jnp.bfloat16
I32 = jnp.int32

EPS = 1e-6
CHUNK_SHIFT = 6
H_A, DH_A = 4, 64
H_B, DH_B = 8, 64
H_I, D_I = 4, 64
TOPK_MAX = 256
H_X = 4
N_GROUPS, EXP_PER_GROUP = 4, 8
N_EXPERTS = N_GROUPS * EXP_PER_GROUP
LAMBDA_INIT = 0.8 - 0.6 * 1.0
LANES = 128
NEG = -1e30
INT_MIN = -2147483648
INT_MAX = 2147483647
KEY_NEG_INF = -2139095041
VMEM_LIMIT = 56 * 1024 * 1024

LOG2E = 1.4426950408889634
TK = 256
WIT_ROWS = 16
IN_WIDTHS = (512, 512, 512, 512, 64, 64, 256, 64, 4)


def _rms(x, g):
    ms = jnp.mean(x * x, axis=-1, keepdims=True)
    return (x * lax.rsqrt(ms + EPS)) * g


def _dot_nt(a, b):
    return lax.dot_general(a, b, (((1,), (1,)), ((), ())), preferred_element_type=F32)


def _dot(a, b):
    return jnp.dot(a, b, preferred_element_type=F32)


def _proj_kernel(x_ref, g_ref, wn_ref, wt_ref, qa_ref, ka_ref, qb_ref, kb_ref, qi_ref, kik_ref,
                 vat_ref, vbt_ref, wit_ref):
    xn = _rms(x_ref[...], g_ref[...]).astype(BF16)
    col = 0
    for ref in (qa_ref, ka_ref, qb_ref, kb_ref, qi_ref, kik_ref):
        n = ref.shape[-1]
        ref[...] = _dot(xn, wn_ref[:, col:col + n]).astype(ref.dtype)
        col += n
    row = 0
    for ref in (vat_ref, vbt_ref):
        r = ref.shape[1]
        yt = _dot_nt(wt_ref[row:row + r, :], xn)
        for c in range(ref.shape[0]):
            ref[c] = yt[:, c * TK:(c + 1) * TK].astype(ref.dtype)
        row += r
    wit_ref[...] = _dot_nt(wt_ref[row:row + wit_ref.shape[0], :], xn)


def _proj(x2, gain, w_n, w_t, tm):
    n, d = x2.shape
    kb = tm // TK
    tok = lambda w: pl.BlockSpec((tm, w), lambda i: (i, 0))
    widths = (H_A * 2 * DH_A, H_A * 2 * DH_A, H_B * DH_B, LANES, H_I * D_I, LANES)
    return pl.pallas_call(
        _proj_kernel,
        grid=(n // tm,),
        in_specs=[pl.BlockSpec((tm, d), lambda i: (i, 0)),
                  pl.BlockSpec((1, d), lambda i: (0, 0)),
                  pl.BlockSpec(w_n.shape, lambda i: (0, 0)),
                  pl.BlockSpec(w_t.shape, lambda i: (0, 0))],
        out_specs=[tok(w) for w in widths] + [
            pl.BlockSpec((kb, H_A * 2 * DH_A, TK), lambda i: (i, 0, 0)),
            pl.BlockSpec((kb, DH_B, TK), lambda i: (i, 0, 0)),
            pl.BlockSpec((WIT_ROWS, tm), lambda i: (0, i))],
        out_shape=[jax.ShapeDtypeStruct((n, w), BF16) for w in widths] + [
            jax.ShapeDtypeStruct((n // TK, H_A * 2 * DH_A, TK), BF16),
            jax.ShapeDtypeStruct((n // TK, DH_B, TK), BF16),
            jax.ShapeDtypeStruct((WIT_ROWS, n), F32)],
        compiler_params=pltpu.CompilerParams(dimension_semantics=("arbitrary",),
                                             vmem_limit_bytes=VMEM_LIMIT),
        name="proj",
    )(x2, gain, w_n, w_t)


def _aug_query(sig_ref, h, tq):
    lane = lax.broadcasted_iota(I32, (tq, LANES), 1)
    i = lax.broadcasted_iota(I32, (tq, LANES), 0).astype(F32)
    s1, s2, s3 = sig_ref[3 * h], sig_ref[3 * h + 1], sig_ref[3 * h + 2]
    c = ((s1 + s2) + s3) * i
    c1 = c.astype(BF16).astype(F32)
    c2 = (c - c1).astype(BF16).astype(F32)
    c3 = (c - c1) - c2
    out = jnp.zeros((tq, LANES), F32)
    for n, v in enumerate((c1, c2, c3, s1, s2, s3, s1, s2, s3, -s1, -s2, -s3)):
        out = jnp.where(lane == n, v, out)
    return out.astype(BF16)


def _aug_queries(sig, n_heads, tq):
    return jnp.stack([_aug_query(sig, h, tq) for h in range(n_heads)])


def _aug_key(tk, off):
    lane = lax.broadcasted_iota(I32, (tk, LANES), 1)
    j = lax.broadcasted_iota(I32, (tk, LANES), 0)
    j_lo = (j & 255).astype(F32)
    j_hi = (j - (j & 255)).astype(F32)
    base = jnp.where(lane < 3, -1.0, jnp.where(lane < 6, j_lo, jnp.where(lane < 9, j_hi, jnp.where(lane < 12, off, 0.0))))
    return base.astype(BF16)


def _sigma(sig_ref, h):
    return (sig_ref[3 * h] + sig_ref[3 * h + 1]) + sig_ref[3 * h + 2]


def _softmax_step(r, st, vt, m_sc, l_sc, acc_sc):
    m_prev = m_sc[r][0:1, :]
    m_next = jnp.maximum(m_prev, jnp.max(st, axis=0, keepdims=True))
    p = jnp.exp2(st - m_next)
    alpha = jnp.exp2(m_prev - m_next)
    l_next = alpha * l_sc[r][0:1, :] + jnp.sum(p, axis=0, keepdims=True)
    acc_sc[r] = alpha * acc_sc[r] + _dot(vt, p.astype(BF16))
    m_sc[r] = jnp.broadcast_to(m_next, m_sc.shape[1:])
    l_sc[r] = jnp.broadcast_to(l_next, l_sc.shape[1:])


def _diff_kernel(sig_ref, lam_ref, q_ref, k_ref, vt_ref, g_ref, augq_ref, o_ref,
                 qaug_sc, s_sc, m_sc, l_sc, acc_sc, *, tq, tk):
    qblk = pl.program_id(1)
    lam = lam_ref[0]
    lane_k = lax.broadcasted_iota(I32, (tk, LANES), 1)
    for h in range(H_A):
        qaug_sc[h] = jnp.concatenate([q_ref[:, h * LANES:(h + 1) * LANES], augq_ref[h]], axis=1)
    m_sc[...] = jnp.full(m_sc.shape, NEG, F32)
    l_sc[...] = jnp.zeros(l_sc.shape, F32)
    acc_sc[...] = jnp.zeros(acc_sc.shape, F32)

    def block(kb, diag):
        off = pl.multiple_of(kb * tk, tk)
        k = k_ref[pl.ds(off, tk), :]
        rel = qblk * tq - kb * tk
        augk = _aug_key(tk, rel.astype(F32))
        if diag:
            jj = lax.broadcasted_iota(I32, (tk, tq), 0)
            ii = lax.broadcasted_iota(I32, (tk, tq), 1)
            allowed = ((off + jj) >> CHUNK_SHIFT) <= ((qblk * tq + ii) >> CHUNK_SHIFT)
            ahead = jnp.maximum(jj - ii - rel, 0).astype(F32)
        for h in range(H_A):
            kh = k[:, h * LANES:(h + 1) * LANES]
            zero = jnp.zeros_like(kh)
            qa = qaug_sc[h]
            for m in range(2):
                km = jnp.where(lane_k < DH_A if m == 0 else lane_k >= DH_A, kh, zero)
                st = _dot_nt(jnp.concatenate([km, augk], axis=1), qa)
                if diag:
                    st = jnp.where(allowed, st - (2.0 * _sigma(sig_ref, h)) * ahead, NEG)
                s_sc[2 * h + m] = st
        for h in range(H_A):
            vt = vt_ref[kb, h * LANES:(h + 1) * LANES, :]
            for m in range(2):
                _softmax_step(2 * h + m, s_sc[2 * h + m], vt, m_sc, l_sc, acc_sc)

    def off_diagonal(kb, carry):
        block(kb, False)
        return carry

    n_before = (qblk * tq) // tk
    lax.fori_loop(0, n_before, off_diagonal, 0)
    for c in range(max(1, tq // tk)):
        block(n_before + c, True)

    gain = jnp.concatenate([g_ref[...]] * (tq // LANES), axis=1)
    for h in range(H_A):
        y = acc_sc[2 * h] / l_sc[2 * h][0:1, :] - lam * (acc_sc[2 * h + 1] / l_sc[2 * h + 1][0:1, :])
        ms = jnp.mean(y * y, axis=0, keepdims=True)
        yn = (y * lax.rsqrt(ms + EPS)) * gain
        o_ref[:, h * LANES:(h + 1) * LANES] = (jnp.transpose(yn) * (1.0 - LAMBDA_INIT)).astype(o_ref.dtype)


def _diff_attention(qa, ka, vat, sig, lam, subln_t, tq, tk):
    b, s, w = qa.shape
    assert tq % tk == 0 or tk % tq == 0
    kern = functools.partial(_diff_kernel, tq=tq, tk=tk)
    smem = pl.BlockSpec(memory_space=pltpu.SMEM)
    return pl.pallas_call(
        kern,
        grid=(b, s // tq),
        in_specs=[smem, smem,
                  pl.BlockSpec((None, tq, w), lambda bi, i: (bi, i, 0)),
                  pl.BlockSpec((None, s, w), lambda bi, i: (bi, 0, 0)),
                  pl.BlockSpec((None, s // tk, w, tk), lambda bi, i: (bi, 0, 0, 0)),
                  pl.BlockSpec((LANES, LANES), lambda bi, i: (0, 0)),
                  pl.BlockSpec((H_A, tq, LANES), lambda bi, i: (0, 0, 0))],
        out_specs=pl.BlockSpec((None, tq, w), lambda bi, i: (bi, i, 0)),
        out_shape=jax.ShapeDtypeStruct((b, s, w), BF16),
        scratch_shapes=[pltpu.VMEM((H_A, tq, 2 * LANES), BF16),
                        pltpu.VMEM((2 * H_A, tk, tq), F32),
                        pltpu.VMEM((2 * H_A, 8, tq), F32),
                        pltpu.VMEM((2 * H_A, 8, tq), F32),
                        pltpu.VMEM((2 * H_A, LANES, tq), F32)],
        compiler_params=pltpu.CompilerParams(dimension_semantics=("arbitrary",) * 2,
                                             vmem_limit_bytes=VMEM_LIMIT),
        name="diff_attn",
    )(sig, lam, qa, ka, vat, subln_t, _aug_queries(sig, H_A, tq))


def _dsa_kernel(sig_ref, qb_ref, kb_ref, vt_ref, qi_ref, kik_ref, wit_ref, augq_ref, o_ref,
                qaug_sc, key_sc, v16_sc, s_sc, m_sc, l_sc, acc_sc, cut_sc, *, tq, tk, k_sel, idx_bits):
    qblk = pl.program_id(1)
    n_before = (qblk * tq) // tk
    nvis = n_before + max(1, tq // tk)
    lane_k = lax.broadcasted_iota(I32, (tk, LANES), 1)
    jj = lax.broadcasted_iota(I32, (tk, tq), 0)
    ii = lax.broadcasted_iota(I32, (tk, tq), 1)
    t_chunk = (qblk * tq + ii) >> CHUNK_SHIFT
    w_idx = wit_ref[...]
    qidx = qi_ref[...]

    def index_block(kb, carry):
        off = pl.multiple_of(kb * tk, tk)
        kik = kik_ref[pl.ds(off, tk), :]
        zero = jnp.zeros_like(kik)
        k_half = (jnp.where(lane_k < D_I, kik, zero), jnp.where(lane_k >= D_I, kik, zero))
        isc = jnp.zeros((tk, tq), F32)
        for h in range(H_I):
            d = _dot_nt(k_half[h % 2], qidx[:, (h // 2) * LANES:(h // 2 + 1) * LANES])
            isc = isc + w_idx[h:h + 1, :] * jnp.maximum(d, 0.0)
        bits = lax.bitcast_convert_type(isc, I32)
        key = jnp.where(bits < 0, bits ^ 0x7FFFFFFF, bits)
        key = jnp.where(isc == 0.0, 0, key)
        allowed = ((off + jj) >> CHUNK_SHIFT) <= t_chunk
        key_sc[kb] = jnp.where(allowed, key, KEY_NEG_INF)
        v16_sc[kb] = jnp.where(allowed, isc, -jnp.inf).astype(BF16)
        return carry

    lax.fori_loop(0, nvis, index_block, 0)

    kf = float(k_sel)
    zero_i = jnp.zeros((1, tq), I32)

    def bf16_bits_of(k16):
        return lax.shift_left(jnp.where(k16 >= 0, k16, k16 ^ 0x7FFF) & 0xFFFF, 16)

    def count16_ge(k16):
        cand = lax.bitcast_convert_type(bf16_bits_of(k16), F32).astype(BF16)
        one = jnp.ones((tk, tq), BF16)
        zero = jnp.zeros((tk, tq), BF16)

        def body(kb, acc):
            hit = jnp.where(v16_sc[kb] >= cand, one, zero)
            for r in range(tk // 16):
                acc = acc + hit[r * 16:(r + 1) * 16]
            return acc
        acc = lax.fori_loop(0, nvis, body, jnp.zeros((16, tq), BF16))
        return jnp.sum(acc.astype(F32), axis=0, keepdims=True)

    t16 = jnp.where(count16_ge(zero_i) >= kf, zero_i, jnp.full((1, tq), -32768, I32))

    def bit16_step(i, t):
        cand = t + lax.shift_left(jnp.int32(1), jnp.asarray(14 - i, I32))
        return jnp.where(count16_ge(cand) >= kf, cand, t)

    t16 = lax.fori_loop(0, 15, bit16_step, t16)

    def count_ge(cand):
        def body(kb, acc):
            hit = jnp.where(key_sc[kb] >= cand, 1.0, 0.0)
            return acc + jnp.sum(hit.reshape(tk // 8, 8, tq), axis=0)
        acc = lax.fori_loop(0, nvis, body, jnp.zeros((8, tq), F32))
        return jnp.sum(acc, axis=0, keepdims=True)

    centre_bits = bf16_bits_of(t16)
    centre = jnp.where(centre_bits < 0, centre_bits ^ 0x7FFFFFFF, centre_bits)
    lo0 = centre - 65536
    hi0 = centre + 65536
    c_lo0 = count_ge(lo0)
    lo0 = jnp.where(c_lo0 >= kf, lo0, INT_MIN)
    c_lo0 = jnp.where(c_lo0 >= kf, c_lo0, float(2 * tq * 8))
    hi0 = jnp.where(count_ge(hi0) < kf, hi0, INT_MAX)

    def open_rows(lo, hi, c_lo):
        return jnp.max(jnp.where((c_lo == kf) | (lo + 1 >= hi), 0.0, 1.0)) > 0.0

    def bisect(state):
        i, lo, hi, c_lo = state
        for _ in range(2):
            mid = (lo >> 1) + (hi >> 1) + (lo & hi & 1)
            c = count_ge(mid)
            keep = c >= kf
            lo, hi, c_lo = jnp.where(keep, mid, lo), jnp.where(keep, hi, mid), jnp.where(keep, c, c_lo)
        return i + 2, lo, hi, c_lo

    _, thr, _, c_thr = lax.while_loop(lambda st: (st[0] < 34) & open_rows(st[1], st[2], st[3]), bisect,
                                      (jnp.int32(0), lo0, hi0, c_lo0))

    need = kf - count_ge(thr + 1)
    surplus = jnp.where(thr == KEY_NEG_INF, 0.0, c_thr - kf)
    thr_eq = jnp.where(thr == KEY_NEG_INF, INT_MIN, thr)
    cut_sc[...] = jnp.full(cut_sc.shape, (1 << idx_bits) - 1, I32)

    @pl.when(jnp.max(surplus) > 0.0)
    def _():
        def count_eq_before(cand):
            def body(kb, acc):
                hit = jnp.where(kb * tk + jj < cand, 1.0, 0.0)
                return acc + jnp.sum(jnp.where(key_sc[kb] == thr_eq, hit, 0.0), axis=0, keepdims=True)
            return lax.fori_loop(0, nvis, body, jnp.zeros((1, tq), F32))

        def idx_step(i, p):
            cand = p + lax.shift_left(jnp.int32(1), jnp.asarray(idx_bits - 1 - i, I32))
            return jnp.where(count_eq_before(cand) < need, cand, p)

        cut_sc[0:1, :] = lax.fori_loop(0, idx_bits, idx_step, zero_i)

    last_eq = cut_sc[0:1, :]

    for h in range(H_B):
        qaug_sc[h] = jnp.concatenate([qb_ref[:, (h // 2) * LANES:(h // 2 + 1) * LANES], augq_ref[h]], axis=1)
    m_sc[...] = jnp.full(m_sc.shape, NEG, F32)
    l_sc[...] = jnp.zeros(l_sc.shape, F32)
    acc_sc[...] = jnp.zeros(acc_sc.shape, F32)

    def attend(kb, diag):
        off = pl.multiple_of(kb * tk, tk)
        kk = kb_ref[pl.ds(off, tk), :]
        zero = jnp.zeros_like(kk)
        rel = qblk * tq - kb * tk
        augk = _aug_key(tk, rel.astype(F32))
        k_aug = (jnp.concatenate([jnp.where(lane_k < DH_B, kk, zero), augk], axis=1),
                 jnp.concatenate([jnp.where(lane_k >= DH_B, kk, zero), augk], axis=1))
        vt = vt_ref[kb]
        key = key_sc[kb]
        keep_tie = jnp.where(off + jj <= last_eq, 0.0, NEG)
        mask = jnp.where(key > thr, 0.0, jnp.where(key == thr_eq, keep_tie, NEG))
        if diag:
            ahead = jnp.maximum(jj - ii - rel, 0).astype(F32)
        for h in range(H_B):
            st = _dot_nt(k_aug[h % 2], qaug_sc[h]) + mask
            if diag:
                st = st - (2.0 * _sigma(sig_ref, h)) * ahead
            s_sc[h] = st
        for h in range(H_B):
            _softmax_step(h, s_sc[h], vt, m_sc, l_sc, acc_sc)

    def off_diagonal(kb, carry):
        attend(kb, False)
        return carry

    lax.fori_loop(0, n_before, off_diagonal, 0)
    for c in range(max(1, tq // tk)):
        attend(n_before + c, True)

    yt = jnp.concatenate([acc_sc[h] / l_sc[h][0:1, :] for h in range(H_B)], axis=0)
    o_ref[...] = jnp.transpose(yt).astype(o_ref.dtype)


def _dsa_attention(qb, kb2, vbt, qi, kik, wit, sig, k_sel, tq, tk):
    b, s, w = qb.shape
    assert tq % tk == 0 or tk % tq == 0
    idx_bits = max(1, (s - 1).bit_length())
    kern = functools.partial(_dsa_kernel, tq=tq, tk=tk, k_sel=k_sel, idx_bits=idx_bits)
    tile = lambda c: pl.BlockSpec((None, tq, c), lambda bi, i: (bi, i, 0))
    full = lambda c: pl.BlockSpec((None, s, c), lambda bi, i: (bi, 0, 0))
    nq = s // tq
    return pl.pallas_call(
        kern,
        grid=(b, nq),
        in_specs=[pl.BlockSpec(memory_space=pltpu.SMEM),
                  tile(w), full(LANES),
                  pl.BlockSpec((None, s // tk, DH_B, tk), lambda bi, i: (bi, 0, 0, 0)),
                  tile(H_I * D_I), full(LANES),
                  pl.BlockSpec((WIT_ROWS, tq), lambda bi, i: (0, bi * nq + i)),
                  pl.BlockSpec((H_B, tq, LANES), lambda bi, i: (0, 0, 0))],
        out_specs=tile(w),
        out_shape=jax.ShapeDtypeStruct((b, s, w), BF16),
        scratch_shapes=[pltpu.VMEM((H_B, tq, 2 * LANES), BF16),
                        pltpu.VMEM((s // tk, tk, tq), I32),
                        pltpu.VMEM((s // tk, tk, tq), BF16),
                        pltpu.VMEM((H_B, tk, tq), F32),
                        pltpu.VMEM((H_B, 8, tq), F32),
                        pltpu.VMEM((H_B, 8, tq), F32),
                        pltpu.VMEM((H_B, DH_B, tq), F32),
                        pltpu.VMEM((8, tq), I32)],
        compiler_params=pltpu.CompilerParams(dimension_semantics=("arbitrary",) * 2,
                                             vmem_limit_bytes=VMEM_LIMIT),
        name="dsa_attn",
    )(sig, qb, kb2, vbt, qi, kik, wit, _aug_queries(sig, H_B, tq))


def _memkv_kernel(m_ref, g_ref, w_ref, o_ref):
    mn = _rms(m_ref[...], g_ref[...]).astype(BF16)
    o_ref[...] = _dot(mn, w_ref[...]).astype(o_ref.dtype)


def _memkv(mem2, gain, w_ckv, tm):
    n, d = mem2.shape
    return pl.pallas_call(
        _memkv_kernel,
        grid=(n // tm,),
        in_specs=[pl.BlockSpec((tm, d), lambda i: (i, 0)),
                  pl.BlockSpec((1, d), lambda i: (0, 0)),
                  pl.BlockSpec(w_ckv.shape, lambda i: (0, 0))],
        out_specs=pl.BlockSpec((tm, w_ckv.shape[1]), lambda i: (i, 0)),
        out_shape=jax.ShapeDtypeStruct((n, w_ckv.shape[1]), BF16),
        compiler_params=pltpu.CompilerParams(dimension_semantics=("arbitrary",),
                                             vmem_limit_bytes=VMEM_LIMIT),
        name="memkv",
    )(mem2, gain, w_ckv)


def _merge_kernel(x_ref, ya_ref, yb_ref, mkv_ref, mixg_ref, wgate_ref, bgate_ref, wa_ref, wb_ref, wout_ref,
                  crossg_ref, wcq_ref, wco_ref, ffng_ref, wrh_ref, wrl_ref, h2_ref, lg_ref):
    d = x_ref.shape[-1]
    dh = d // H_X
    x = x_ref[...]
    xn = _rms(x, mixg_ref[...]).astype(BF16)
    gates = jax.nn.sigmoid(_dot(xn, wgate_ref[...]) + bgate_ref[...])
    merged = gates[:, :d] * _dot(ya_ref[...], wa_ref[...]) + gates[:, d:] * _dot(yb_ref[...], wb_ref[...])
    h1 = x + _dot(merged.astype(BF16), wout_ref[...])

    q = _dot(_rms(h1, crossg_ref[...]).astype(BF16), wcq_ref[...]).astype(BF16)
    heads = []
    for h in range(H_X):
        k = mkv_ref[:, h * dh:(h + 1) * dh]
        v = mkv_ref[:, d + h * dh:d + (h + 1) * dh]
        s = _dot_nt(q[:, h * dh:(h + 1) * dh], k)
        e = jnp.exp(s - jnp.max(s, axis=1, keepdims=True))
        p = e / jnp.sum(e, axis=1, keepdims=True)
        heads.append(_dot(p.astype(BF16), v).astype(BF16))
    h2 = h1 + _dot(jnp.concatenate(heads, axis=1), wco_ref[...])
    h2_ref[...] = h2

    f = _rms(h2, ffng_ref[...])
    f_hi = f.astype(BF16)
    f_lo = (f - f_hi.astype(F32)).astype(BF16)
    w_hi = wrh_ref[...]
    lg_ref[...] = _dot_nt(w_hi, f_hi) + _dot_nt(w_hi, f_lo) + _dot_nt(wrl_ref[...], f_hi)


def _merge(x3, ya, yb, mkv, p, tm):
    b, s, d = x3.shape
    m = mkv.shape[1]
    tok = lambda w: pl.BlockSpec((None, tm, w), lambda bi, i: (bi, i, 0))
    const = lambda a: pl.BlockSpec(a.shape, lambda bi, i: (0,) * a.ndim, pipeline_mode=pl.Buffered(1))
    consts = (p["mix_g"], p["w_gate"], p["b_gate"], p["w_a"], p["w_b"], p["w_out"], p["cross_g"],
              p["w_cq"], p["w_co"], p["ffn_g"], p["w_r_hi"], p["w_r_lo"])
    return pl.pallas_call(
        _merge_kernel,
        grid=(b, s // tm),
        in_specs=[tok(d), tok(ya.shape[-1]), tok(yb.shape[-1]),
                  pl.BlockSpec((None, m, 2 * d), lambda bi, i: (bi, 0, 0))] + [const(a) for a in consts],
        out_specs=[tok(d), pl.BlockSpec((LANES, tm), lambda bi, i: (0, bi * (s // tm) + i))],
        out_shape=[jax.ShapeDtypeStruct((b, s, d), F32), jax.ShapeDtypeStruct((LANES, b * s), F32)],
        compiler_params=pltpu.CompilerParams(dimension_semantics=("arbitrary",) * 2,
                                             vmem_limit_bytes=VMEM_LIMIT),
        name="merge",
    )(x3, ya, yb, mkv, *consts)


def _route_kernel(lg_ref, bias_ref, mi_ref, mf_ref, cnt_ref, carry_sc, *, tm):
    @pl.when(pl.program_id(0) == 0)
    def _():
        carry_sc[...] = jnp.zeros(carry_sc.shape, F32)

    lg = lg_ref[...] + bias_ref[...]
    e = lg[0:N_EXPERTS]
    g = lg[N_EXPERTS:N_EXPERTS + 8]
    row_g = lax.broadcasted_iota(I32, (8, tm), 0)
    g = jnp.where(row_g < N_GROUPS, g, -jnp.inf)
    gmax = jnp.max(g, axis=0, keepdims=True)
    g_idx = jnp.min(jnp.where(g == gmax, row_g, N_GROUPS), axis=0, keepdims=True)
    p_g = 1.0 / jnp.sum(jnp.exp(g - gmax), axis=0, keepdims=True)

    row_e = lax.broadcasted_iota(I32, (N_EXPERTS, tm), 0)
    in_grp = (row_e >> 3) == g_idx
    emax = jnp.max(jnp.where(in_grp, e, -jnp.inf), axis=0, keepdims=True)
    ex = jnp.where(in_grp, jnp.exp(jnp.where(in_grp, e - emax, 0.0)), 0.0)
    probs = ex / jnp.sum(ex, axis=0, keepdims=True)
    big = N_EXPERTS
    p1 = jnp.max(probs, axis=0, keepdims=True)
    i1 = jnp.min(jnp.where(in_grp, jnp.where(probs == p1, row_e, big), big), axis=0, keepdims=True)
    probs2 = jnp.where(in_grp, jnp.where(row_e == i1, -1.0, probs), -1.0)
    p2 = jnp.max(probs2, axis=0, keepdims=True)
    i2 = jnp.min(jnp.where(probs2 == p2, row_e, big), axis=0, keepdims=True)
    denom = p1 + p2
    gate1 = p_g * p1 / denom
    gate2 = p_g * p2 / denom

    used = jnp.where(row_e == i1, 1.0, jnp.where(row_e == i2, 1.0, 0.0))
    r = lax.broadcasted_iota(I32, (tm, tm), 0)
    c = lax.broadcasted_iota(I32, (tm, tm), 1)
    before = jnp.where(r < c, 1.0, 0.0).astype(BF16)
    excl = _dot(used.astype(BF16), before) + carry_sc[:, 0:1]
    rank1 = jnp.sum(jnp.where(row_e == i1, excl, 0.0), axis=0, keepdims=True)
    rank2 = jnp.sum(jnp.where(row_e == i2, excl, 0.0), axis=0, keepdims=True)
    carry_sc[...] = carry_sc[...] + jnp.sum(used, axis=1, keepdims=True)
    cnt_ref[...] = carry_sc[...]

    mi_ref[...] = jnp.zeros(mi_ref.shape, I32)
    mi_ref[0:1, :] = i1
    mi_ref[1:2, :] = i2
    mi_ref[2:3, :] = rank1.astype(I32)
    mi_ref[3:4, :] = rank2.astype(I32)
    mf_ref[...] = jnp.zeros(mf_ref.shape, F32)
    mf_ref[0:1, :] = gate1
    mf_ref[1:2, :] = gate2


def _route(lg_t, bias, tm):
    n = lg_t.shape[1]
    kern = functools.partial(_route_kernel, tm=tm)
    return pl.pallas_call(
        kern,
        grid=(n // tm,),
        in_specs=[pl.BlockSpec((LANES, tm), lambda i: (0, i)),
                  pl.BlockSpec((LANES, 1), lambda i: (0, 0))],
        out_specs=[pl.BlockSpec((8, tm), lambda i: (0, i)),
                   pl.BlockSpec((8, tm), lambda i: (0, i)),
                   pl.BlockSpec((N_EXPERTS, LANES), lambda i: (0, 0))],
        out_shape=[jax.ShapeDtypeStruct((8, n), I32), jax.ShapeDtypeStruct((8, n), F32),
                   jax.ShapeDtypeStruct((N_EXPERTS, LANES), F32)],
        scratch_shapes=[pltpu.VMEM((N_EXPERTS, LANES), F32)],
        compiler_params=pltpu.CompilerParams(dimension_semantics=("arbitrary",),
                                             vmem_limit_bytes=VMEM_LIMIT),
        name="route",
    )(lg_t, bias)


ROW_DMA_UNROLL = 8


def _dispatch_kernel(dest_ref, h2_ref, xs_in_hbm, xs_hbm, sem, *, tm):
    del xs_in_hbm

    def row_copy(t0, u, dst_row):
        src = h2_ref.at[pl.ds(t0, ROW_DMA_UNROLL)].at[pl.ds(u, 1)]
        return pltpu.make_async_copy(src, xs_hbm.at[pl.ds(dst_row, 1)], sem)

    def issue(c, carry):
        t0 = pl.multiple_of(c * ROW_DMA_UNROLL, ROW_DMA_UNROLL)
        for u in range(ROW_DMA_UNROLL):
            for j in range(2):
                row_copy(t0, u, dest_ref[0, j * tm + t0 + u]).start()
        return carry

    def drain(t, carry):
        for j in range(2):
            row_copy(0, 0, 0).wait()
        return carry

    lax.fori_loop(0, tm // ROW_DMA_UNROLL, issue, 0)
    lax.fori_loop(0, tm, drain, 0, unroll=ROW_DMA_UNROLL)


def _dispatch(dest, h2, n_slots, tm):
    n, d = h2.shape
    kern = functools.partial(_dispatch_kernel, tm=tm)
    xs0 = jnp.zeros((n_slots, d), F32)
    return pl.pallas_call(
        kern,
        grid=(n // tm,),
        in_specs=[pl.BlockSpec((None, 1, 2 * tm), lambda i: (i, 0, 0), memory_space=pltpu.SMEM),
                  pl.BlockSpec((tm, d), lambda i: (i, 0)),
                  pl.BlockSpec(memory_space=pl.ANY)],
        out_specs=pl.BlockSpec(memory_space=pl.ANY),
        out_shape=jax.ShapeDtypeStruct((n_slots, d), F32),
        scratch_shapes=[pltpu.SemaphoreType.DMA(())],
        input_output_aliases={2: 0},
        compiler_params=pltpu.CompilerParams(dimension_semantics=("arbitrary",),
                                             has_side_effects=True, vmem_limit_bytes=VMEM_LIMIT),
        name="dispatch",
    )(dest, h2, xs0)


def _expert_kernel(be_ref, nu_ref, x_ref, g_ref, w1_ref, w3_ref, w2_ref, y_ref):
    del be_ref

    @pl.when(pl.program_id(0) < nu_ref[0])
    def _():
        xb = _rms(x_ref[...], g_ref[...]).astype(BF16)
        a = _dot(xb, w1_ref[...])
        hb = (a * jax.nn.sigmoid(a)) * _dot(xb, w3_ref[...])
        y_ref[...] = _dot(hb.astype(BF16), w2_ref[...])

    @pl.when(pl.program_id(0) >= nu_ref[0])
    def _():
        y_ref[...] = jnp.zeros(y_ref.shape, y_ref.dtype)


def _experts(block_expert, n_used, xs, gain, w1, w3, w2, tb):
    n_slots, d = xs.shape
    de = w1.shape[-1]
    nb = n_slots // tb
    row = lambda i, be, nu: (i, 0)
    grid_spec = pltpu.PrefetchScalarGridSpec(
        num_scalar_prefetch=2,
        grid=(nb,),
        in_specs=[pl.BlockSpec((tb, d), row),
                  pl.BlockSpec((1, d), lambda i, be, nu: (0, 0)),
                  pl.BlockSpec((None, d, de), lambda i, be, nu: (be[i], 0, 0)),
                  pl.BlockSpec((None, d, de), lambda i, be, nu: (be[i], 0, 0)),
                  pl.BlockSpec((None, de, d), lambda i, be, nu: (be[i], 0, 0))],
        out_specs=pl.BlockSpec((tb, d), row),
    )
    return pl.pallas_call(
        _expert_kernel,
        grid_spec=grid_spec,
        out_shape=jax.ShapeDtypeStruct((n_slots, d), F32),
        compiler_params=pltpu.CompilerParams(dimension_semantics=("arbitrary",),
                                             vmem_limit_bytes=VMEM_LIMIT),
        name="experts",
    )(block_expert, n_used, xs, gain, w1, w3, w2)


def _combine_kernel(dest_ref, mf_ref, h2_ref, g_ref, y_hbm, o_ref, ybuf, sem, *, tm):
    def row_copy(src_row, j, t0, u):
        dst = ybuf.at[j, pl.ds(t0, ROW_DMA_UNROLL)].at[pl.ds(u, 1)]
        return pltpu.make_async_copy(y_hbm.at[pl.ds(src_row, 1)], dst, sem)

    def issue(c, carry):
        t0 = pl.multiple_of(c * ROW_DMA_UNROLL, ROW_DMA_UNROLL)
        for u in range(ROW_DMA_UNROLL):
            for j in range(2):
                row_copy(dest_ref[0, j * tm + t0 + u], j, t0, u).start()
        return carry

    def drain(t, carry):
        for j in range(2):
            row_copy(0, j, 0, 0).wait()
        return carry

    lax.fori_loop(0, tm // ROW_DMA_UNROLL, issue, 0)
    lax.fori_loop(0, tm, drain, 0, unroll=ROW_DMA_UNROLL)

    gates = jnp.concatenate([mf_ref[...], jnp.zeros((LANES - 8, tm), F32)], axis=0)
    gt = jnp.transpose(gates)
    h = h2_ref[...] + gt[:, 0:1] * ybuf[0] + gt[:, 1:2] * ybuf[1]
    o_ref[...] = _rms(h, g_ref[...])


def _combine(dest, mf, h2, gain, y, tm):
    n, d = h2.shape
    kern = functools.partial(_combine_kernel, tm=tm)
    return pl.pallas_call(
        kern,
        grid=(n // tm,),
        in_specs=[pl.BlockSpec((None, 1, 2 * tm), lambda i: (i, 0, 0), memory_space=pltpu.SMEM),
                  pl.BlockSpec((8, tm), lambda i: (0, i)),
                  pl.BlockSpec((tm, d), lambda i: (i, 0)),
                  pl.BlockSpec((1, d), lambda i: (0, 0)),
                  pl.BlockSpec(memory_space=pl.ANY)],
        out_specs=pl.BlockSpec((tm, d), lambda i: (i, 0)),
        out_shape=jax.ShapeDtypeStruct((n, d), F32),
        scratch_shapes=[pltpu.VMEM((2, tm, d), F32), pltpu.SemaphoreType.DMA(())],
        compiler_params=pltpu.CompilerParams(dimension_semantics=("arbitrary",),
                                             vmem_limit_bytes=VMEM_LIMIT),
        name="combine",
    )(dest, mf, h2, gain, y)


def _sigma_parts(n_heads):
    sigma = jnp.asarray([2.0 ** (-8.0 * (i + 1) / n_heads) for i in range(n_heads)], dtype=F32) * LOG2E
    s1 = sigma.astype(BF16).astype(F32)
    s2 = (sigma - s1).astype(BF16).astype(F32)
    s3 = ((sigma - s1) - s2).astype(BF16).astype(F32)
    return jnp.stack([s1, s2, s3], axis=1).reshape(-1)


def _prep_w_in(w_in):
    splits = []
    acc = 0
    for w in IN_WIDTHS[:-1]:
        acc += w
        splits.append(acc)
    qa, ka, va, qb, kb, vb, qi, ki, wi = jnp.split(w_in, splits, axis=1)
    w_n = jnp.concatenate([qa * (DH_A ** -0.5 * LOG2E), ka, qb * (DH_B ** -0.5 * LOG2E), kb, kb, qi, ki, ki], axis=1)
    pad = jnp.zeros((w_in.shape[0], WIT_ROWS - IN_WIDTHS[-1]), w_in.dtype)
    w_t = jnp.concatenate([va, vb, wi * (H_I ** -0.5 * D_I ** -0.5), pad], axis=1).T
    return w_n.astype(BF16), w_t.astype(BF16)


def _prep_router(w_group, w_router):
    d = w_group.shape[0]
    w = jnp.concatenate([w_router, w_group, jnp.zeros((d, LANES - N_EXPERTS - N_GROUPS), F32)], axis=1).T
    hi = w.astype(BF16)
    lo = (w - hi.astype(F32)).astype(BF16)
    return hi, lo


def _block_sizes(s):
    return dict(tm_proj=512, tq=256, tk=TK, tm_merge=min(512, s), tm_route=512, tm_disp=512, tb=512, tm_comb=256)


def kernel(x, mem, mix_norm, w_in, lam_q1, lam_k1, lam_q2, lam_k2, diff_subln, w_branch_a, w_branch_b, w_gate,
           b_gate, w_out, cross_norm, mem_norm, w_cq, w_ckv, w_co, ffn_norm, w_group, b_group, w_router,
           b_router, w1, w3, w2, final_norm):
    b, s, d = x.shape
    n = b * s
    m = mem.shape[1]
    bs = _block_sizes(s)
    k_sel = min(TOPK_MAX, s // 4)
    row = lambda v: v.reshape(1, -1).astype(F32)

    w_n, w_t = _prep_w_in(w_in[0])
    qa, ka, qb, kb2, qi, kik, vat, vbt, wit = _proj(x.reshape(n, d), row(mix_norm[0]), w_n, w_t, bs["tm_proj"])
    qa, ka, qb, kb2, qi, kik = [o.reshape(b, s, -1) for o in (qa, ka, qb, kb2, qi, kik)]
    vat = vat.reshape(b, s // TK, -1, TK)
    vbt = vbt.reshape(b, s // TK, -1, TK)
    lam = (jnp.exp(jnp.sum(lam_q1[0].astype(F32) * lam_k1[0].astype(F32)))
           - jnp.exp(jnp.sum(lam_q2[0].astype(F32) * lam_k2[0].astype(F32))) + LAMBDA_INIT).reshape(1)
    subln_t = jnp.broadcast_to(diff_subln[0].astype(F32)[:, None], (LANES, LANES))
    ya = _diff_attention(qa, ka, vat, _sigma_parts(H_A), lam, subln_t, bs["tq"], bs["tk"])
    yb = _dsa_attention(qb, kb2, vbt, qi, kik, wit, _sigma_parts(H_B), k_sel, bs["tq"], bs["tk"])

    mkv = _memkv(mem.reshape(b * m, d), row(mem_norm[0]), w_ckv[0].astype(BF16), min(512, b * m))
    w_r_hi, w_r_lo = _prep_router(w_group[0], w_router[0])
    params = dict(mix_g=row(mix_norm[0]), w_gate=w_gate[0].astype(BF16), b_gate=row(b_gate[0]),
                  w_a=w_branch_a[0].astype(BF16), w_b=w_branch_b[0].astype(BF16), w_out=w_out[0].astype(BF16),
                  cross_g=row(cross_norm[0]), w_cq=(w_cq[0] * (d // H_X) ** -0.5).astype(BF16),
                  w_co=w_co[0].astype(BF16), ffn_g=row(ffn_norm[0]), w_r_hi=w_r_hi, w_r_lo=w_r_lo)
    h2, lg_t = _merge(x, ya, yb, mkv.reshape(b, m, 2 * d), params, bs["tm_merge"])
    h2 = h2.reshape(n, d)

    bias = jnp.concatenate([b_router[0], b_group[0], jnp.zeros((LANES - N_EXPERTS - N_GROUPS,), F32)])
    mi, mf, cnt = _route(lg_t, bias.reshape(LANES, 1).astype(F32), bs["tm_route"])
    tb = bs["tb"]
    counts = cnt[:, 0].astype(I32)
    padded = (counts + tb - 1) // tb * tb
    pad_end = jnp.cumsum(padded)
    pad_start = (pad_end - padded).astype(I32)
    n_blocks = (2 * n) // tb + N_EXPERTS
    blk_first = jnp.arange(n_blocks, dtype=I32) * tb
    block_expert = jnp.minimum(jnp.sum((pad_end[None, :] <= blk_first[:, None]).astype(I32), axis=1), N_EXPERTS - 1)
    n_used = (pad_end[-1] // tb).astype(I32).reshape(1)
    start_of = jnp.sum(jnp.where(mi[0:2][None] == jnp.arange(N_EXPERTS, dtype=I32)[:, None, None],
                                 pad_start[:, None, None], 0), axis=0)
    dest = start_of + mi[2:4]

    def per_tile(tm):
        return dest.reshape(2, n // tm, tm).transpose(1, 0, 2).reshape(n // tm, 1, 2 * tm)

    xs = _dispatch(per_tile(bs["tm_disp"]), h2, n_blocks * tb, bs["tm_disp"])
    y = _experts(block_expert, n_used, xs, row(ffn_norm[0]), w1[0].astype(BF16), w3[0].astype(BF16),
                 w2[0].astype(BF16), tb)
    out = _combine(per_tile(bs["tm_comb"]), mf, h2, row(final_norm), y, bs["tm_comb"])
    return out.reshape(b, s, d)
```

```python
import functools

import jax
import jax.numpy as jnp
from jax import lax
from jax.experimental import pallas as pl
from jax.experimental.pallas import tpu as pltpu

F32 = jnp.float32
BF16 = jnp.bfloat16
I32 = jnp.int32

EPS = 1e-6
CHUNK_SHIFT = 6
H_A, DH_A = 4, 64
H_B, DH_B = 8, 64
H_I, D_I = 4, 64
TOPK_MAX = 256
H_X = 4
N_GROUPS, EXP_PER_GROUP = 4, 8
N_EXPERTS = N_GROUPS * EXP_PER_GROUP
LAMBDA_INIT = 0.8 - 0.6 * 1.0
LANES = 128
NEG = -1e30
INT_MIN = -2147483648
INT_MAX = 2147483647
KEY_NEG_INF = -2139095041
VMEM_LIMIT = 56 * 1024 * 1024

LOG2E = 1.4426950408889634
TK = 256
WIT_ROWS = 16
IN_WIDTHS = (512, 512, 512, 512, 64, 64, 256, 64, 4)


def _rms(x, g):
    ms = jnp.mean(x * x, axis=-1, keepdims=True)
    return (x * lax.rsqrt(ms + EPS)) * g


def _dot_nt(a, b):
    return lax.dot_general(a, b, (((1,), (1,)), ((), ())), preferred_element_type=F32)


def _dot(a, b):
    return jnp.dot(a, b, preferred_element_type=F32)


def _proj_kernel(x_ref, g_ref, wn_ref, wt_ref, qa_ref, ka_ref, qb_ref, kb_ref, qi_ref, kik_ref,
                 vat_ref, vbt_ref, wit_ref):
    xn = _rms(x_ref[...], g_ref[...]).astype(BF16)
    col = 0
    for ref in (qa_ref, ka_ref, qb_ref, kb_ref, qi_ref, kik_ref):
        n = ref.shape[-1]
        ref[...] = _dot(xn, wn_ref[:, col:col + n]).astype(ref.dtype)
        col += n
    row = 0
    for ref in (vat_ref, vbt_ref):
        r = ref.shape[1]
        yt = _dot_nt(wt_ref[row:row + r, :], xn)
        for c in range(ref.shape[0]):
            ref[c] = yt[:, c * TK:(c + 1) * TK].astype(ref.dtype)
        row += r
    wit_ref[...] = _dot_nt(wt_ref[row:row + wit_ref.shape[0], :], xn)


def _proj(x2, gain, w_n, w_t, tm):
    n, d = x2.shape
    kb = tm // TK
    tok = lambda w: pl.BlockSpec((tm, w), lambda i: (i, 0))
    widths = (H_A * 2 * DH_A, H_A * 2 * DH_A, H_B * DH_B, LANES, H_I * D_I, LANES)
    return pl.pallas_call(
        _proj_kernel,
        grid=(n // tm,),
        in_specs=[pl.BlockSpec((tm, d), lambda i: (i, 0)),
                  pl.BlockSpec((1, d), lambda i: (0, 0)),
                  pl.BlockSpec(w_n.shape, lambda i: (0, 0)),
                  pl.BlockSpec(w_t.shape, lambda i: (0, 0))],
        out_specs=[tok(w) for w in widths] + [
            pl.BlockSpec((kb, H_A * 2 * DH_A, TK), lambda i: (i, 0, 0)),
            pl.BlockSpec((kb, DH_B, TK), lambda i: (i, 0, 0)),
            pl.BlockSpec((WIT_ROWS, tm), lambda i: (0, i))],
        out_shape=[jax.ShapeDtypeStruct((n, w), BF16) for w in widths] + [
            jax.ShapeDtypeStruct((n // TK, H_A * 2 * DH_A, TK), BF16),
            jax.ShapeDtypeStruct((n // TK, DH_B, TK), BF16),
            jax.ShapeDtypeStruct((WIT_ROWS, n), F32)],
        compiler_params=pltpu.CompilerParams(dimension_semantics=("arbitrary",),
                                             vmem_limit_bytes=VMEM_LIMIT),
        name="proj",
    )(x2, gain, w_n, w_t)


def _aug_query(sig_ref, h, tq):
    lane = lax.broadcasted_iota(I32, (tq, LANES), 1)
    i = lax.broadcasted_iota(I32, (tq, LANES), 0).astype(F32)
    s1, s2, s3 = sig_ref[3 * h], sig_ref[3 * h + 1], sig_ref[3 * h + 2]
    c = ((s1 + s2) + s3) * i
    c1 = c.astype(BF16).astype(F32)
    c2 = (c - c1).astype(BF16).astype(F32)
    c3 = (c - c1) - c2
    out = jnp.zeros((tq, LANES), F32)
    for n, v in enumerate((c1, c2, c3, s1, s2, s3, s1, s2, s3, -s1, -s2, -s3)):
        out = jnp.where(lane == n, v, out)
    return out.astype(BF16)


def _aug_queries(sig, n_heads, tq):
    return jnp.stack([_aug_query(sig, h, tq) for h in range(n_heads)])


def _aug_key(tk, off):
    lane = lax.broadcasted_iota(I32, (tk, LANES), 1)
    j = lax.broadcasted_iota(I32, (tk, LANES), 0)
    j_lo = (j & 255).astype(F32)
    j_hi = (j - (j & 255)).astype(F32)
    base = jnp.where(lane < 3, -1.0, jnp.where(lane < 6, j_lo, jnp.where(lane < 9, j_hi, jnp.where(lane < 12, off, 0.0))))
    return base.astype(BF16)


def _sigma(sig_ref, h):
    return (sig_ref[3 * h] + sig_ref[3 * h + 1]) + sig_ref[3 * h + 2]


def _softmax_step(r, st, vt, m_sc, l_sc, acc_sc):
    m_prev = m_sc[r][0:1, :]
    m_next = jnp.maximum(m_prev, jnp.max(st, axis=0, keepdims=True))
    p = jnp.exp2(st - m_next)
    alpha = jnp.exp2(m_prev - m_next)
    l_next = alpha * l_sc[r][0:1, :] + jnp.sum(p, axis=0, keepdims=True)
    acc_sc[r] = alpha * acc_sc[r] + _dot(vt, p.astype(BF16))
    m_sc[r] = jnp.broadcast_to(m_next, m_sc.shape[1:])
    l_sc[r] = jnp.broadcast_to(l_next, l_sc.shape[1:])


def _diff_kernel(sig_ref, lam_ref, q_ref, k_ref, vt_ref, g_ref, augq_ref, o_ref,
                 qaug_sc, s_sc, m_sc, l_sc, acc_sc, *, tq, tk):
    qblk = pl.program_id(1)
    lam = lam_ref[0]
    lane_k = lax.broadcasted_iota(I32, (tk, LANES), 1)
    for h in range(H_A):
        qaug_sc[h] = jnp.concatenate([q_ref[:, h * LANES:(h + 1) * LANES], augq_ref[h]], axis=1)
    m_sc[...] = jnp.full(m_sc.shape, NEG, F32)
    l_sc[...] = jnp.zeros(l_sc.shape, F32)
    acc_sc[...] = jnp.zeros(acc_sc.shape, F32)

    def block(kb, diag):
        off = pl.multiple_of(kb * tk, tk)
        k = k_ref[pl.ds(off, tk), :]
        rel = qblk * tq - kb * tk
        augk = _aug_key(tk, rel.astype(F32))
        if diag:
            jj = lax.broadcasted_iota(I32, (tk, tq), 0)
            ii = lax.broadcasted_iota(I32, (tk, tq), 1)
            allowed = ((off + jj) >> CHUNK_SHIFT) <= ((qblk * tq + ii) >> CHUNK_SHIFT)
            ahead = jnp.maximum(jj - ii - rel, 0).astype(F32)
        for h in range(H_A):
            kh = k[:, h * LANES:(h + 1) * LANES]
            zero = jnp.zeros_like(kh)
            qa = qaug_sc[h]
            for m in range(2):
                km = jnp.where(lane_k < DH_A if m == 0 else lane_k >= DH_A, kh, zero)
                st = _dot_nt(jnp.concatenate([km, augk], axis=1), qa)
                if diag:
                    st = jnp.where(allowed, st - (2.0 * _sigma(sig_ref, h)) * ahead, NEG)
                s_sc[2 * h + m] = st
        for h in range(H_A):
            vt = vt_ref[kb, h * LANES:(h + 1) * LANES, :]
            for m in range(2):
                _softmax_step(2 * h + m, s_sc[2 * h + m], vt, m_sc, l_sc, acc_sc)

    def off_diagonal(kb, carry):
        block(kb, False)
        return carry

    n_before = (qblk * tq) // tk
    lax.fori_loop(0, n_before, off_diagonal, 0)
    for c in range(max(1, tq // tk)):
        block(n_before + c, True)

    gain = jnp.concatenate([g_ref[...]] * (tq // LANES), axis=1)
    for h in range(H_A):
        y = acc_sc[2 * h] / l_sc[2 * h][0:1, :] - lam * (acc_sc[2 * h + 1] / l_sc[2 * h + 1][0:1, :])
        ms = jnp.mean(y * y, axis=0, keepdims=True)
        yn = (y * lax.rsqrt(ms + EPS)) * gain
        o_ref[:, h * LANES:(h + 1) * LANES] = (jnp.transpose(yn) * (1.0 - LAMBDA_INIT)).astype(o_ref.dtype)


def _diff_attention(qa, ka, vat, sig, lam, subln_t, tq, tk):
    b, s, w = qa.shape
    assert tq % tk == 0 or tk % tq == 0
    kern = functools.partial(_diff_kernel, tq=tq, tk=tk)
    smem = pl.BlockSpec(memory_space=pltpu.SMEM)
    return pl.pallas_call(
        kern,
        grid=(b, s // tq),
        in_specs=[smem, smem,
                  pl.BlockSpec((None, tq, w), lambda bi, i: (bi, i, 0)),
                  pl.BlockSpec((None, s, w), lambda bi, i: (bi, 0, 0)),
                  pl.BlockSpec((None, s // tk, w, tk), lambda bi, i: (bi, 0, 0, 0)),
                  pl.BlockSpec((LANES, LANES), lambda bi, i: (0, 0)),
                  pl.BlockSpec((H_A, tq, LANES), lambda bi, i: (0, 0, 0))],
        out_specs=pl.BlockSpec((None, tq, w), lambda bi, i: (bi, i, 0)),
        out_shape=jax.ShapeDtypeStruct((b, s, w), BF16),
        scratch_shapes=[pltpu.VMEM((H_A, tq, 2 * LANES), BF16),
                        pltpu.VMEM((2 * H_A, tk, tq), F32),
                        pltpu.VMEM((2 * H_A, 8, tq), F32),
                        pltpu.VMEM((2 * H_A, 8, tq), F32),
                        pltpu.VMEM((2 * H_A, LANES, tq), F32)],
        compiler_params=pltpu.CompilerParams(dimension_semantics=("arbitrary",) * 2,
                                             vmem_limit_bytes=VMEM_LIMIT),
        name="diff_attn",
    )(sig, lam, qa, ka, vat, subln_t, _aug_queries(sig, H_A, tq))


def _dsa_kernel(sig_ref, qb_ref, kb_ref, vt_ref, qi_ref, kik_ref, wit_ref, augq_ref, tri_ref, o_ref,
                qaug_sc, key_sc, s_sc, m_sc, l_sc, acc_sc, *, tq, tk, k_sel):
    qblk = pl.program_id(1)
    n_before = (qblk * tq) // tk
    nvis = n_before + max(1, tq // tk)
    lane_k = lax.broadcasted_iota(I32, (tk, LANES), 1)
    jj = lax.broadcasted_iota(I32, (tk, tq), 0)
    ii = lax.broadcasted_iota(I32, (tk, tq), 1)
    t_chunk = (qblk * tq + ii) >> CHUNK_SHIFT
    w_idx = wit_ref[...]
    qidx = qi_ref[...]

    def index_block(kb, carry):
        off = pl.multiple_of(kb * tk, tk)
        kik = kik_ref[pl.ds(off, tk), :]
        zero = jnp.zeros_like(kik)
        k_half = (jnp.where(lane_k < D_I, kik, zero), jnp.where(lane_k >= D_I, kik, zero))
        isc = jnp.zeros((tk, tq), F32)
        for h in range(H_I):
            d = _dot_nt(k_half[h % 2], qidx[:, (h // 2) * LANES:(h // 2 + 1) * LANES])
            isc = isc + w_idx[h:h + 1, :] * jnp.maximum(d, 0.0)
        bits = lax.bitcast_convert_type(isc, I32)
        key = jnp.where(bits < 0, bits ^ 0x7FFFFFFF, bits)
        key = jnp.where(isc == 0.0, 0, key)
        allowed = ((off + jj) >> CHUNK_SHIFT) <= t_chunk
        key_sc[kb] = jnp.where(allowed, key, KEY_NEG_INF)
        return carry

    lax.fori_loop(0, nvis, index_block, 0)

    def count_ge(cand):
        def body(kb, acc):
            hit = jnp.where(key_sc[kb] >= cand, 1.0, 0.0)
            return acc + jnp.sum(hit.reshape(tk // 8, 8, tq), axis=0)
        acc = lax.fori_loop(0, nvis, body, jnp.zeros((8, tq), F32))
        return jnp.sum(acc, axis=0, keepdims=True)

    kf = float(k_sel)
    zero_i = jnp.zeros((1, tq), I32)
    thr = jnp.where(count_ge(zero_i) >= kf, zero_i, jnp.full((1, tq), INT_MIN, I32))

    def bit_step(i, t):
        cand = t + lax.shift_left(jnp.int32(1), jnp.asarray(30 - i, I32))
        return jnp.where(count_ge(cand) >= kf, cand, t)

    thr = lax.fori_loop(0, 31, bit_step, thr)

    need = kf - count_ge(thr + 1)
    thr_eq = jnp.where(thr == KEY_NEG_INF, INT_MIN, thr)

    for h in range(H_B):
        qaug_sc[h] = jnp.concatenate([qb_ref[:, (h // 2) * LANES:(h // 2 + 1) * LANES], augq_ref[h]], axis=1)
    m_sc[...] = jnp.full(m_sc.shape, NEG, F32)
    l_sc[...] = jnp.zeros(l_sc.shape, F32)
    acc_sc[...] = jnp.zeros(acc_sc.shape, F32)

    def attend(kb, diag, ties_before):
        off = pl.multiple_of(kb * tk, tk)
        kk = kb_ref[pl.ds(off, tk), :]
        zero = jnp.zeros_like(kk)
        rel = qblk * tq - kb * tk
        augk = _aug_key(tk, rel.astype(F32))
        k_aug = (jnp.concatenate([jnp.where(lane_k < DH_B, kk, zero), augk], axis=1),
                 jnp.concatenate([jnp.where(lane_k >= DH_B, kk, zero), augk], axis=1))
        vt = vt_ref[kb]
        key = key_sc[kb]
        tie = key == thr_eq
        tie_rank = _dot(tri_ref[...], jnp.where(tie, 1.0, 0.0).astype(BF16)) + ties_before
        mask = jnp.where(key > thr, 0.0, jnp.where(tie, jnp.where(tie_rank <= need, 0.0, NEG), NEG))
        if diag:
            ahead = jnp.maximum(jj - ii - rel, 0).astype(F32)
        for h in range(H_B):
            st = _dot_nt(k_aug[h % 2], qaug_sc[h]) + mask
            if diag:
                st = st - (2.0 * _sigma(sig_ref, h)) * ahead
            s_sc[h] = st
        for h in range(H_B):
            _softmax_step(h, s_sc[h], vt, m_sc, l_sc, acc_sc)
        return tie_rank[tk - 1:tk, :]

    ties = lax.fori_loop(0, n_before, lambda kb, t: attend(kb, False, t), jnp.zeros((1, tq), F32))
    for c in range(max(1, tq // tk)):
        ties = attend(n_before + c, True, ties)

    yt = jnp.concatenate([acc_sc[h] / l_sc[h][0:1, :] for h in range(H_B)], axis=0)
    o_ref[...] = jnp.transpose(yt).astype(o_ref.dtype)


def _dsa_attention(qb, kb2, vbt, qi, kik, wit, sig, k_sel, tq, tk):
    b, s, w = qb.shape
    assert tq % tk == 0 or tk % tq == 0
    kern = functools.partial(_dsa_kernel, tq=tq, tk=tk, k_sel=k_sel)
    tri = (jnp.arange(tk)[None, :] <= jnp.arange(tk)[:, None]).astype(BF16)
    tile = lambda c: pl.BlockSpec((None, tq, c), lambda bi, i: (bi, i, 0))
    full = lambda c: pl.BlockSpec((None, s, c), lambda bi, i: (bi, 0, 0))
    nq = s // tq
    return pl.pallas_call(
        kern,
        grid=(b, nq),
        in_specs=[pl.BlockSpec(memory_space=pltpu.SMEM),
                  tile(w), full(LANES),
                  pl.BlockSpec((None, s // tk, DH_B, tk), lambda bi, i: (bi, 0, 0, 0)),
                  tile(H_I * D_I), full(LANES),
                  pl.BlockSpec((WIT_ROWS, tq), lambda bi, i: (0, bi * nq + i)),
                  pl.BlockSpec((H_B, tq, LANES), lambda bi, i: (0, 0, 0)),
                  pl.BlockSpec((tk, tk), lambda bi, i: (0, 0))],
        out_specs=tile(w),
        out_shape=jax.ShapeDtypeStruct((b, s, w), BF16),
        scratch_shapes=[pltpu.VMEM((H_B, tq, 2 * LANES), BF16),
                        pltpu.VMEM((s // tk, tk, tq), I32),
                        pltpu.VMEM((H_B, tk, tq), F32),
                        pltpu.VMEM((H_B, 8, tq), F32),
                        pltpu.VMEM((H_B, 8, tq), F32),
                        pltpu.VMEM((H_B, DH_B, tq), F32)],
        compiler_params=pltpu.CompilerParams(dimension_semantics=("arbitrary",) * 2,
                                             vmem_limit_bytes=VMEM_LIMIT),
        name="dsa_attn",
    )(sig, qb, kb2, vbt, qi, kik, wit, _aug_queries(sig, H_B, tq), tri)


def _memkv_kernel(m_ref, g_ref, w_ref, o_ref):
    mn = _rms(m_ref[...], g_ref[...]).astype(BF16)
    o_ref[...] = _dot(mn, w_ref[...]).astype(o_ref.dtype)


def _memkv(mem2, gain, w_ckv, tm):
    n, d = mem2.shape
    return pl.pallas_call(
        _memkv_kernel,
        grid=(n // tm,),
        in_specs=[pl.BlockSpec((tm, d), lambda i: (i, 0)),
                  pl.BlockSpec((1, d), lambda i: (0, 0)),
                  pl.BlockSpec(w_ckv.shape, lambda i: (0, 0))],
        out_specs=pl.BlockSpec((tm, w_ckv.shape[1]), lambda i: (i, 0)),
        out_shape=jax.ShapeDtypeStruct((n, w_ckv.shape[1]), BF16),
        compiler_params=pltpu.CompilerParams(dimension_semantics=("arbitrary",),
                                             vmem_limit_bytes=VMEM_LIMIT),
        name="memkv",
    )(mem2, gain, w_ckv)


def _merge_kernel(x_ref, ya_ref, yb_ref, mkv_ref, mixg_ref, wgate_ref, bgate_ref, wa_ref, wb_ref, wout_ref,
                  crossg_ref, wcq_ref, wco_ref, ffng_ref, wrh_ref, wrl_ref, h2_ref, lg_ref):
    d = x_ref.shape[-1]
    dh = d // H_X
    x = x_ref[...]
    xn = _rms(x, mixg_ref[...]).astype(BF16)
    gates = jax.nn.sigmoid(_dot(xn, wgate_ref[...]) + bgate_ref[...])
    merged = gates[:, :d] * _dot(ya_ref[...], wa_ref[...]) + gates[:, d:] * _dot(yb_ref[...], wb_ref[...])
    h1 = x + _dot(merged.astype(BF16), wout_ref[...])

    q = _dot(_rms(h1, crossg_ref[...]).astype(BF16), wcq_ref[...]).astype(BF16)
    heads = []
    for h in range(H_X):
        k = mkv_ref[:, h * dh:(h + 1) * dh]
        v = mkv_ref[:, d + h * dh:d + (h + 1) * dh]
        s = _dot_nt(q[:, h * dh:(h + 1) * dh], k)
        e = jnp.exp(s - jnp.max(s, axis=1, keepdims=True))
        p = e / jnp.sum(e, axis=1, keepdims=True)
        heads.append(_dot(p.astype(BF16), v).astype(BF16))
    h2 = h1 + _dot(jnp.concatenate(heads, axis=1), wco_ref[...])
    h2_ref[...] = h2

    f = _rms(h2, ffng_ref[...])
    f_hi = f.astype(BF16)
    f_lo = (f - f_hi.astype(F32)).astype(BF16)
    w_hi = wrh_ref[...]
    lg_ref[...] = _dot_nt(w_hi, f_hi) + _dot_nt(w_hi, f_lo) + _dot_nt(wrl_ref[...], f_hi)


def _merge(x3, ya, yb, mkv, p, tm):
    b, s, d = x3.shape
    m = mkv.shape[1]
    tok = lambda w: pl.BlockSpec((None, tm, w), lambda bi, i: (bi, i, 0))
    const = lambda a: pl.BlockSpec(a.shape, lambda bi, i: (0,) * a.ndim, pipeline_mode=pl.Buffered(1))
    consts = (p["mix_g"], p["w_gate"], p["b_gate"], p["w_a"], p["w_b"], p["w_out"], p["cross_g"],
              p["w_cq"], p["w_co"], p["ffn_g"], p["w_r_hi"], p["w_r_lo"])
    return pl.pallas_call(
        _merge_kernel,
        grid=(b, s // tm),
        in_specs=[tok(d), tok(ya.shape[-1]), tok(yb.shape[-1]),
                  pl.BlockSpec((None, m, 2 * d), lambda bi, i: (bi, 0, 0))] + [const(a) for a in consts],
        out_specs=[tok(d), pl.BlockSpec((LANES, tm), lambda bi, i: (0, bi * (s // tm) + i))],
        out_shape=[jax.ShapeDtypeStruct((b, s, d), F32), jax.ShapeDtypeStruct((LANES, b * s), F32)],
        compiler_params=pltpu.CompilerParams(dimension_semantics=("arbitrary",) * 2,
                                             vmem_limit_bytes=VMEM_LIMIT),
        name="merge",
    )(x3, ya, yb, mkv, *consts)


def _route_kernel(lg_ref, bias_ref, mi_ref, mf_ref, cnt_ref, carry_sc, *, tm):
    @pl.when(pl.program_id(0) == 0)
    def _():
        carry_sc[...] = jnp.zeros(carry_sc.shape, F32)

    lg = lg_ref[...] + bias_ref[...]
    e = lg[0:N_EXPERTS]
    g = lg[N_EXPERTS:N_EXPERTS + 8]
    row_g = lax.broadcasted_iota(I32, (8, tm), 0)
    g = jnp.where(row_g < N_GROUPS, g, -jnp.inf)
    gmax = jnp.max(g, axis=0, keepdims=True)
    g_idx = jnp.min(jnp.where(g == gmax, row_g, N_GROUPS), axis=0, keepdims=True)
    p_g = 1.0 / jnp.sum(jnp.exp(g - gmax), axis=0, keepdims=True)

    row_e = lax.broadcasted_iota(I32, (N_EXPERTS, tm), 0)
    in_grp = (row_e >> 3) == g_idx
    emax = jnp.max(jnp.where(in_grp, e, -jnp.inf), axis=0, keepdims=True)
    ex = jnp.where(in_grp, jnp.exp(jnp.where(in_grp, e - emax, 0.0)), 0.0)
    probs = ex / jnp.sum(ex, axis=0, keepdims=True)
    big = N_EXPERTS
    p1 = jnp.max(probs, axis=0, keepdims=True)
    i1 = jnp.min(jnp.where(in_grp, jnp.where(probs == p1, row_e, big), big), axis=0, keepdims=True)
    probs2 = jnp.where(in_grp, jnp.where(row_e == i1, -1.0, probs), -1.0)
    p2 = jnp.max(probs2, axis=0, keepdims=True)
    i2 = jnp.min(jnp.where(probs2 == p2, row_e, big), axis=0, keepdims=True)
    denom = p1 + p2
    gate1 = p_g * p1 / denom
    gate2 = p_g * p2 / denom

    used = jnp.where(row_e == i1, 1.0, jnp.where(row_e == i2, 1.0, 0.0))
    r = lax.broadcasted_iota(I32, (tm, tm), 0)
    c = lax.broadcasted_iota(I32, (tm, tm), 1)
    before = jnp.where(r < c, 1.0, 0.0).astype(BF16)
    excl = _dot(used.astype(BF16), before) + carry_sc[:, 0:1]
    rank1 = jnp.sum(jnp.where(row_e == i1, excl, 0.0), axis=0, keepdims=True)
    rank2 = jnp.sum(jnp.where(row_e == i2, excl, 0.0), axis=0, keepdims=True)
    carry_sc[...] = carry_sc[...] + jnp.sum(used, axis=1, keepdims=True)
    cnt_ref[...] = carry_sc[...]

    mi_ref[...] = jnp.zeros(mi_ref.shape, I32)
    mi_ref[0:1, :] = i1
    mi_ref[1:2, :] = i2
    mi_ref[2:3, :] = rank1.astype(I32)
    mi_ref[3:4, :] = rank2.astype(I32)
    mf_ref[...] = jnp.zeros(mf_ref.shape, F32)
    mf_ref[0:1, :] = gate1
    mf_ref[1:2, :] = gate2


def _route(lg_t, bias, tm):
    n = lg_t.shape[1]
    kern = functools.partial(_route_kernel, tm=tm)
    return pl.pallas_call(
        kern,
        grid=(n // tm,),
        in_specs=[pl.BlockSpec((LANES, tm), lambda i: (0, i)),
                  pl.BlockSpec((LANES, 1), lambda i: (0, 0))],
        out_specs=[pl.BlockSpec((8, tm), lambda i: (0, i)),
                   pl.BlockSpec((8, tm), lambda i: (0, i)),
                   pl.BlockSpec((N_EXPERTS, LANES), lambda i: (0, 0))],
        out_shape=[jax.ShapeDtypeStruct((8, n), I32), jax.ShapeDtypeStruct((8, n), F32),
                   jax.ShapeDtypeStruct((N_EXPERTS, LANES), F32)],
        scratch_shapes=[pltpu.VMEM((N_EXPERTS, LANES), F32)],
        compiler_params=pltpu.CompilerParams(dimension_semantics=("arbitrary",),
                                             vmem_limit_bytes=VMEM_LIMIT),
        name="route",
    )(lg_t, bias)


ROW_DMA_UNROLL = 8


def _dispatch_kernel(dest_ref, h2_ref, xs_in_hbm, xs_hbm, sem, *, tm):
    del xs_in_hbm

    def row_copy(t0, u, dst_row):
        src = h2_ref.at[pl.ds(t0, ROW_DMA_UNROLL)].at[pl.ds(u, 1)]
        return pltpu.make_async_copy(src, xs_hbm.at[pl.ds(dst_row, 1)], sem)

    def issue(c, carry):
        t0 = pl.multiple_of(c * ROW_DMA_UNROLL, ROW_DMA_UNROLL)
        for u in range(ROW_DMA_UNROLL):
            for j in range(2):
                row_copy(t0, u, dest_ref[0, j * tm + t0 + u]).start()
        return carry

    def drain(t, carry):
        for j in range(2):
            row_copy(0, 0, 0).wait()
        return carry

    lax.fori_loop(0, tm // ROW_DMA_UNROLL, issue, 0)
    lax.fori_loop(0, tm, drain, 0, unroll=ROW_DMA_UNROLL)


def _dispatch(dest, h2, n_slots, tm):
    n, d = h2.shape
    kern = functools.partial(_dispatch_kernel, tm=tm)
    xs0 = jnp.zeros((n_slots, d), F32)
    return pl.pallas_call(
        kern,
        grid=(n // tm,),
        in_specs=[pl.BlockSpec((None, 1, 2 * tm), lambda i: (i, 0, 0), memory_space=pltpu.SMEM),
                  pl.BlockSpec((tm, d), lambda i: (i, 0)),
                  pl.BlockSpec(memory_space=pl.ANY)],
        out_specs=pl.BlockSpec(memory_space=pl.ANY),
        out_shape=jax.ShapeDtypeStruct((n_slots, d), F32),
        scratch_shapes=[pltpu.SemaphoreType.DMA(())],
        input_output_aliases={2: 0},
        compiler_params=pltpu.CompilerParams(dimension_semantics=("arbitrary",),
                                             has_side_effects=True, vmem_limit_bytes=VMEM_LIMIT),
        name="dispatch",
    )(dest, h2, xs0)


def _expert_kernel(be_ref, nu_ref, x_ref, g_ref, w1_ref, w3_ref, w2_ref, y_ref):
    del be_ref

    @pl.when(pl.program_id(0) < nu_ref[0])
    def _():
        xb = _rms(x_ref[...], g_ref[...]).astype(BF16)
        a = _dot(xb, w1_ref[...])
        hb = (a * jax.nn.sigmoid(a)) * _dot(xb, w3_ref[...])
        y_ref[...] = _dot(hb.astype(BF16), w2_ref[...])

    @pl.when(pl.program_id(0) >= nu_ref[0])
    def _():
        y_ref[...] = jnp.zeros(y_ref.shape, y_ref.dtype)


def _experts(block_expert, n_used, xs, gain, w1, w3, w2, tb):
    n_slots, d = xs.shape
    de = w1.shape[-1]
    nb = n_slots // tb
    row = lambda i, be, nu: (i, 0)
    grid_spec = pltpu.PrefetchScalarGridSpec(
        num_scalar_prefetch=2,
        grid=(nb,),
        in_specs=[pl.BlockSpec((tb, d), row),
                  pl.BlockSpec((1, d), lambda i, be, nu: (0, 0)),
                  pl.BlockSpec((None, d, de), lambda i, be, nu: (be[i], 0, 0)),
                  pl.BlockSpec((None, d, de), lambda i, be, nu: (be[i], 0, 0)),
                  pl.BlockSpec((None, de, d), lambda i, be, nu: (be[i], 0, 0))],
        out_specs=pl.BlockSpec((tb, d), row),
    )
    return pl.pallas_call(
        _expert_kernel,
        grid_spec=grid_spec,
        out_shape=jax.ShapeDtypeStruct((n_slots, d), F32),
        compiler_params=pltpu.CompilerParams(dimension_semantics=("arbitrary",),
                                             vmem_limit_bytes=VMEM_LIMIT),
        name="experts",
    )(block_expert, n_used, xs, gain, w1, w3, w2)


def _combine_kernel(dest_ref, mf_ref, h2_ref, g_ref, y_hbm, o_ref, ybuf, sem, *, tm):
    def row_copy(src_row, j, t0, u):
        dst = ybuf.at[j, pl.ds(t0, ROW_DMA_UNROLL)].at[pl.ds(u, 1)]
        return pltpu.make_async_copy(y_hbm.at[pl.ds(src_row, 1)], dst, sem)

    def issue(c, carry):
        t0 = pl.multiple_of(c * ROW_DMA_UNROLL, ROW_DMA_UNROLL)
        for u in range(ROW_DMA_UNROLL):
            for j in range(2):
                row_copy(dest_ref[0, j * tm + t0 + u], j, t0, u).start()
        return carry

    def drain(t, carry):
        for j in range(2):
            row_copy(0, j, 0, 0).wait()
        return carry

    lax.fori_loop(0, tm // ROW_DMA_UNROLL, issue, 0)
    lax.fori_loop(0, tm, drain, 0, unroll=ROW_DMA_UNROLL)

    gates = jnp.concatenate([mf_ref[...], jnp.zeros((LANES - 8, tm), F32)], axis=0)
    gt = jnp.transpose(gates)
    h = h2_ref[...] + gt[:, 0:1] * ybuf[0] + gt[:, 1:2] * ybuf[1]
    o_ref[...] = _rms(h, g_ref[...])


def _combine(dest, mf, h2, gain, y, tm):
    n, d = h2.shape
    kern = functools.partial(_combine_kernel, tm=tm)
    return pl.pallas_call(
        kern,
        grid=(n // tm,),
        in_specs=[pl.BlockSpec((None, 1, 2 * tm), lambda i: (i, 0, 0), memory_space=pltpu.SMEM),
                  pl.BlockSpec((8, tm), lambda i: (0, i)),
                  pl.BlockSpec((tm, d), lambda i: (i, 0)),
                  pl.BlockSpec((1, d), lambda i: (0, 0)),
                  pl.BlockSpec(memory_space=pl.ANY)],
        out_specs=pl.BlockSpec((tm, d), lambda i: (i, 0)),
        out_shape=jax.ShapeDtypeStruct((n, d), F32),
        scratch_shapes=[pltpu.VMEM((2, tm, d), F32), pltpu.SemaphoreType.DMA(())],
        compiler_params=pltpu.CompilerParams(dimension_semantics=("arbitrary",),
                                             vmem_limit_bytes=VMEM_LIMIT),
        name="combine",
    )(dest, mf, h2, gain, y)


def _sigma_parts(n_heads):
    sigma = jnp.asarray([2.0 ** (-8.0 * (i + 1) / n_heads) for i in range(n_heads)], dtype=F32) * LOG2E
    s1 = sigma.astype(BF16).astype(F32)
    s2 = (sigma - s1).astype(BF16).astype(F32)
    s3 = ((sigma - s1) - s2).astype(BF16).astype(F32)
    return jnp.stack([s1, s2, s3], axis=1).reshape(-1)


def _prep_w_in(w_in):
    splits = []
    acc = 0
    for w in IN_WIDTHS[:-1]:
        acc += w
        splits.append(acc)
    qa, ka, va, qb, kb, vb, qi, ki, wi = jnp.split(w_in, splits, axis=1)
    w_n = jnp.concatenate([qa * (DH_A ** -0.5 * LOG2E), ka, qb * (DH_B ** -0.5 * LOG2E), kb, kb, qi, ki, ki], axis=1)
    pad = jnp.zeros((w_in.shape[0], WIT_ROWS - IN_WIDTHS[-1]), w_in.dtype)
    w_t = jnp.concatenate([va, vb, wi * (H_I ** -0.5 * D_I ** -0.5), pad], axis=1).T
    return w_n.astype(BF16), w_t.astype(BF16)


def _prep_router(w_group, w_router):
    d = w_group.shape[0]
    w = jnp.concatenate([w_router, w_group, jnp.zeros((d, LANES - N_EXPERTS - N_GROUPS), F32)], axis=1).T
    hi = w.astype(BF16)
    lo = (w - hi.astype(F32)).astype(BF16)
    return hi, lo


def _block_sizes(s):
    return dict(tm_proj=512, tq=256, tk=TK, tm_merge=min(512, s), tm_route=512, tm_disp=512, tb=512, tm_comb=256)


def kernel(x, mem, mix_norm, w_in, lam_q1, lam_k1, lam_q2, lam_k2, diff_subln, w_branch_a, w_branch_b, w_gate,
           b_gate, w_out, cross_norm, mem_norm, w_cq, w_ckv, w_co, ffn_norm, w_group, b_group, w_router,
           b_router, w1, w3, w2, final_norm):
    b, s, d = x.shape
    n = b * s
    m = mem.shape[1]
    bs = _block_sizes(s)
    k_sel = min(TOPK_MAX, s // 4)
    row = lambda v: v.reshape(1, -1).astype(F32)

    w_n, w_t = _prep_w_in(w_in[0])
    qa, ka, qb, kb2, qi, kik, vat, vbt, wit = _proj(x.reshape(n, d), row(mix_norm[0]), w_n, w_t, bs["tm_proj"])
    qa, ka, qb, kb2, qi, kik = [o.reshape(b, s, -1) for o in (qa, ka, qb, kb2, qi, kik)]
    vat = vat.reshape(b, s // TK, -1, TK)
    vbt = vbt.reshape(b, s // TK, -1, TK)
    lam = (jnp.exp(jnp.sum(lam_q1[0].astype(F32) * lam_k1[0].astype(F32)))
           - jnp.exp(jnp.sum(lam_q2[0].astype(F32) * lam_k2[0].astype(F32))) + LAMBDA_INIT).reshape(1)
    subln_t = jnp.broadcast_to(diff_subln[0].astype(F32)[:, None], (LANES, LANES))
    ya = _diff_attention(qa, ka, vat, _sigma_parts(H_A), lam, subln_t, bs["tq"], bs["tk"])
    yb = _dsa_attention(qb, kb2, vbt, qi, kik, wit, _sigma_parts(H_B), k_sel, bs["tq"], bs["tk"])

    mkv = _memkv(mem.reshape(b * m, d), row(mem_norm[0]), w_ckv[0].astype(BF16), min(512, b * m))
    w_r_hi, w_r_lo = _prep_router(w_group[0], w_router[0])
    params = dict(mix_g=row(mix_norm[0]), w_gate=w_gate[0].astype(BF16), b_gate=row(b_gate[0]),
                  w_a=w_branch_a[0].astype(BF16), w_b=w_branch_b[0].astype(BF16), w_out=w_out[0].astype(BF16),
                  cross_g=row(cross_norm[0]), w_cq=(w_cq[0] * (d // H_X) ** -0.5).astype(BF16),
                  w_co=w_co[0].astype(BF16), ffn_g=row(ffn_norm[0]), w_r_hi=w_r_hi, w_r_lo=w_r_lo)
    h2, lg_t = _merge(x, ya, yb, mkv.reshape(b, m, 2 * d), params, bs["tm_merge"])
    h2 = h2.reshape(n, d)

    bias = jnp.concatenate([b_router[0], b_group[0], jnp.zeros((LANES - N_EXPERTS - N_GROUPS,), F32)])
    mi, mf, cnt = _route(lg_t, bias.reshape(LANES, 1).astype(F32), bs["tm_route"])
    tb = bs["tb"]
    counts = cnt[:, 0].astype(I32)
    padded = (counts + tb - 1) // tb * tb
    pad_end = jnp.cumsum(padded)
    pad_start = (pad_end - padded).astype(I32)
    n_blocks = (2 * n) // tb + N_EXPERTS
    blk_first = jnp.arange(n_blocks, dtype=I32) * tb
    block_expert = jnp.minimum(jnp.sum((pad_end[None, :] <= blk_first[:, None]).astype(I32), axis=1), N_EXPERTS - 1)
    n_used = (pad_end[-1] // tb).astype(I32).reshape(1)
    start_of = jnp.sum(jnp.where(mi[0:2][None] == jnp.arange(N_EXPERTS, dtype=I32)[:, None, None],
                                 pad_start[:, None, None], 0), axis=0)
    dest = start_of + mi[2:4]

    def per_tile(tm):
        return dest.reshape(2, n // tm, tm).transpose(1, 0, 2).reshape(n // tm, 1, 2 * tm)

    xs = _dispatch(per_tile(bs["tm_disp"]), h2, n_blocks * tb, bs["tm_disp"])
    y = _experts(block_expert, n_used, xs, row(ffn_norm[0]), w1[0].astype(BF16), w3[0].astype(BF16),
                 w2[0].astype(BF16), tb)
    out = _combine(per_tile(bs["tm_comb"]), mf, h2, row(final_norm), y, bs["tm_comb"])
    return out.reshape(b, s, d)
```

```python
import functools

import jax
import jax.numpy as jnp
from jax import lax
from jax.experimental import pallas as pl
from jax.experimental.pallas import tpu as pltpu

F32 = jnp.float32
BF16 = jnp.bfloat16
I32 = jnp.int32

EPS = 1e-6
CHUNK_SHIFT = 6
H_A, DH_A = 4, 64
H_B, DH_B = 8, 64
H_I, D_I = 4, 64
TOPK_MAX = 256
H_X = 4
N_GROUPS, EXP_PER_GROUP = 4, 8
N_EXPERTS = N_GROUPS * EXP_PER_GROUP
LAMBDA_INIT = 0.8 - 0.6 * 1.0
LANES = 128
NEG = -1e30
INT_MIN = -2147483648
KEY_NEG_INF = -2139095041
VMEM_LIMIT = 56 * 1024 * 1024

LOG2E = 1.4426950408889634
TK = 256
WIT_ROWS = 16
SUM_ROWS = 16
IN_WIDTHS = (512, 512, 512, 512, 64, 64, 256, 64, 4)


def _rms(x, g):
    ms = jnp.mean(x * x, axis=-1, keepdims=True)
    return (x * lax.rsqrt(ms + EPS)) * g


def _dot_nt(a, b):
    return lax.dot_general(a, b, (((1,), (1,)), ((), ())), preferred_element_type=F32)


def _dot(a, b):
    return jnp.dot(a, b, preferred_element_type=F32)


def _proj_kernel(x_ref, g_ref, wn_ref, wt_ref, ka_ref, kb_ref, kik_ref,
                 qat_ref, qbt_ref, qit_ref, vat_ref, vbt_ref, wit_ref):
    xn = _rms(x_ref[...], g_ref[...]).astype(BF16)
    col = 0
    for ref in (ka_ref, kb_ref, kik_ref):
        n = ref.shape[-1]
        ref[...] = _dot(xn, wn_ref[:, col:col + n]).astype(ref.dtype)
        col += n
    row = 0
    for ref in (qat_ref, qbt_ref, qit_ref, vat_ref, vbt_ref):
        r = ref.shape[1]
        yt = _dot_nt(wt_ref[row:row + r, :], xn)
        for c in range(ref.shape[0]):
            ref[c] = yt[:, c * TK:(c + 1) * TK].astype(ref.dtype)
        row += r
    wit_ref[...] = _dot_nt(wt_ref[row:row + wit_ref.shape[0], :], xn)


def _proj(x2, gain, w_n, w_t, tm):
    n, d = x2.shape
    kb = tm // TK
    tok = lambda w: pl.BlockSpec((tm, w), lambda i: (i, 0))
    widths = (H_A * 2 * DH_A, LANES, LANES)
    rows_t = (H_A * 2 * DH_A, H_B * DH_B, H_I * D_I, H_A * 2 * DH_A, DH_B)
    return pl.pallas_call(
        _proj_kernel,
        grid=(n // tm,),
        in_specs=[pl.BlockSpec((tm, d), lambda i: (i, 0)),
                  pl.BlockSpec((1, d), lambda i: (0, 0)),
                  pl.BlockSpec(w_n.shape, lambda i: (0, 0)),
                  pl.BlockSpec(w_t.shape, lambda i: (0, 0))],
        out_specs=[tok(w) for w in widths] + [pl.BlockSpec((kb, r, TK), lambda i: (i, 0, 0)) for r in rows_t] + [
            pl.BlockSpec((WIT_ROWS, tm), lambda i: (0, i))],
        out_shape=[jax.ShapeDtypeStruct((n, w), BF16) for w in widths] + [
            jax.ShapeDtypeStruct((n // TK, r, TK), BF16) for r in rows_t] + [
            jax.ShapeDtypeStruct((WIT_ROWS, n), F32)],
        compiler_params=pltpu.CompilerParams(dimension_semantics=("arbitrary",),
                                             vmem_limit_bytes=VMEM_LIMIT),
        name="proj",
    )(x2, gain, w_n, w_t)


def _aug_query(sig_ref, h, tq):
    lane = lax.broadcasted_iota(I32, (tq, LANES), 1)
    i = lax.broadcasted_iota(I32, (tq, LANES), 0).astype(F32)
    s1, s2, s3 = sig_ref[3 * h], sig_ref[3 * h + 1], sig_ref[3 * h + 2]
    c = ((s1 + s2) + s3) * i
    c1 = c.astype(BF16).astype(F32)
    c2 = (c - c1).astype(BF16).astype(F32)
    c3 = (c - c1) - c2
    out = jnp.zeros((tq, LANES), F32)
    for n, v in enumerate((c1, c2, c3, s1, s2, s3, s1, s2, s3, -s1, -s2, -s3)):
        out = jnp.where(lane == n, v, out)
    return out.astype(BF16)


def _aug_queries(sig, n_heads, tq):
    return jnp.stack([_aug_query(sig, h, tq).T for h in range(n_heads)])


def _aug_key(tk, off):
    lane = lax.broadcasted_iota(I32, (tk, LANES), 1)
    j = lax.broadcasted_iota(I32, (tk, LANES), 0)
    j_lo = (j & 255).astype(F32)
    j_hi = (j - (j & 255)).astype(F32)
    base = jnp.where(lane < 3, -1.0, jnp.where(lane < 6, j_lo, jnp.where(lane < 9, j_hi, jnp.where(lane < 12, off, 0.0))))
    return base.astype(BF16)


def _sigma(sig_ref, h):
    return (sig_ref[3 * h] + sig_ref[3 * h + 1]) + sig_ref[3 * h + 2]


def _with_ones_rows(vt):
    return jnp.concatenate([vt, jnp.ones((SUM_ROWS, vt.shape[1]), vt.dtype)], axis=0)


def _softmax_step(r, st, vt_ones, m_sc, acc_sc):
    m_prev = m_sc[r][0:1, :]
    m_next = jnp.maximum(m_prev, jnp.max(st, axis=0, keepdims=True))
    p = jnp.exp2(st - m_next)
    alpha = jnp.exp2(m_prev - m_next)
    acc_sc[r] = alpha * acc_sc[r] + _dot(vt_ones, p.astype(BF16))
    m_sc[r] = jnp.broadcast_to(m_next, m_sc.shape[1:])


def _diff_kernel(sig_ref, lam_ref, q_ref, k_ref, vt_ref, g_ref, augq_ref, o_ref,
                 qaug_sc, s_sc, m_sc, acc_sc, *, tq, tk):
    qblk = pl.program_id(1)
    lam = lam_ref[0]
    lane_k = lax.broadcasted_iota(I32, (tk, LANES), 1)
    for h in range(H_A):
        qaug_sc[h] = jnp.concatenate([q_ref[h * LANES:(h + 1) * LANES, :], augq_ref[h]], axis=0)
    m_sc[...] = jnp.full(m_sc.shape, NEG, F32)
    acc_sc[...] = jnp.zeros(acc_sc.shape, F32)

    def block(kb, diag):
        off = pl.multiple_of(kb * tk, tk)
        k = k_ref[pl.ds(off, tk), :]
        rel = qblk * tq - kb * tk
        augk = _aug_key(tk, rel.astype(F32))
        if diag:
            jj = lax.broadcasted_iota(I32, (tk, tq), 0)
            ii = lax.broadcasted_iota(I32, (tk, tq), 1)
            allowed = ((off + jj) >> CHUNK_SHIFT) <= ((qblk * tq + ii) >> CHUNK_SHIFT)
            ahead = jnp.maximum(jj - ii - rel, 0).astype(F32)
        for h in range(H_A):
            kh = k[:, h * LANES:(h + 1) * LANES]
            zero = jnp.zeros_like(kh)
            qa = qaug_sc[h]
            for m in range(2):
                km = jnp.where(lane_k < DH_A if m == 0 else lane_k >= DH_A, kh, zero)
                st = _dot(jnp.concatenate([km, augk], axis=1), qa)
                if diag:
                    st = jnp.where(allowed, st - (2.0 * _sigma(sig_ref, h)) * ahead, NEG)
                s_sc[2 * h + m] = st
        for h in range(H_A):
            vt = _with_ones_rows(vt_ref[kb, h * LANES:(h + 1) * LANES, :])
            for m in range(2):
                _softmax_step(2 * h + m, s_sc[2 * h + m], vt, m_sc, acc_sc)

    def off_diagonal(kb, carry):
        block(kb, False)
        return carry

    n_before = (qblk * tq) // tk
    lax.fori_loop(0, n_before, off_diagonal, 0)
    for c in range(max(1, tq // tk)):
        block(n_before + c, True)

    gain = jnp.concatenate([g_ref[...]] * (tq // LANES), axis=1)
    for h in range(H_A):
        a0, a1 = acc_sc[2 * h], acc_sc[2 * h + 1]
        y = a0[:LANES] / a0[LANES:LANES + 1] - lam * (a1[:LANES] / a1[LANES:LANES + 1])
        ms = jnp.mean(y * y, axis=0, keepdims=True)
        yn = (y * lax.rsqrt(ms + EPS)) * gain
        o_ref[:, h * LANES:(h + 1) * LANES] = (jnp.transpose(yn) * (1.0 - LAMBDA_INIT)).astype(o_ref.dtype)


def _diff_attention(qat, ka, vat, sig, lam, subln_t, tq, tk):
    b, s, w = ka.shape
    assert tq % tk == 0 or tk % tq == 0
    kern = functools.partial(_diff_kernel, tq=tq, tk=tk)
    smem = pl.BlockSpec(memory_space=pltpu.SMEM)
    return pl.pallas_call(
        kern,
        grid=(b, s // tq),
        in_specs=[smem, smem,
                  pl.BlockSpec((None, None, w, tq), lambda bi, i: (bi, i, 0, 0)),
                  pl.BlockSpec((None, s, w), lambda bi, i: (bi, 0, 0)),
                  pl.BlockSpec((None, s // tk, w, tk), lambda bi, i: (bi, 0, 0, 0)),
                  pl.BlockSpec((LANES, LANES), lambda bi, i: (0, 0)),
                  pl.BlockSpec((H_A, LANES, tq), lambda bi, i: (0, 0, 0))],
        out_specs=pl.BlockSpec((None, tq, w), lambda bi, i: (bi, i, 0)),
        out_shape=jax.ShapeDtypeStruct((b, s, w), BF16),
        scratch_shapes=[pltpu.VMEM((H_A, 2 * LANES, tq), BF16),
                        pltpu.VMEM((2 * H_A, tk, tq), F32),
                        pltpu.VMEM((2 * H_A, 8, tq), F32),
                        pltpu.VMEM((2 * H_A, LANES + SUM_ROWS, tq), F32)],
        compiler_params=pltpu.CompilerParams(dimension_semantics=("arbitrary",) * 2,
                                             vmem_limit_bytes=VMEM_LIMIT),
        name="diff_attn",
    )(sig, lam, qat, ka, vat, subln_t, _aug_queries(sig, H_A, tq))


def _dsa_kernel(sig_ref, qb_ref, kb_ref, vt_ref, qi_ref, kik_ref, wit_ref, augq_ref, tri_ref, o_ref,
                qaug_sc, key_sc, v16_sc, thr_sc, s_sc, m_sc, acc_sc, *, tq, tk, k_sel):
    qblk = pl.program_id(1)
    n_before = (qblk * tq) // tk
    nvis = n_before + max(1, tq // tk)
    lane_k = lax.broadcasted_iota(I32, (tk, LANES), 1)
    jj = lax.broadcasted_iota(I32, (tk, tq), 0)
    ii = lax.broadcasted_iota(I32, (tk, tq), 1)
    t_chunk = (qblk * tq + ii) >> CHUNK_SHIFT
    w_idx = wit_ref[...]
    qidx = qi_ref[...]

    def index_block(kb, carry):
        off = pl.multiple_of(kb * tk, tk)
        kik = kik_ref[pl.ds(off, tk), :]
        zero = jnp.zeros_like(kik)
        k_half = (jnp.where(lane_k < D_I, kik, zero), jnp.where(lane_k >= D_I, kik, zero))
        isc = jnp.zeros((tk, tq), F32)
        for h in range(H_I):
            d = _dot(k_half[h % 2], qidx[(h // 2) * LANES:(h // 2 + 1) * LANES, :])
            isc = isc + w_idx[h:h + 1, :] * jnp.maximum(d, 0.0)
        bits = lax.bitcast_convert_type(isc, I32)
        key = jnp.where(bits < 0, bits ^ 0x7FFFFFFF, bits)
        key = jnp.where(isc == 0.0, 0, key)
        allowed = ((off + jj) >> CHUNK_SHIFT) <= t_chunk
        key = jnp.where(allowed, key, KEY_NEG_INF)
        key_sc[kb] = key
        hi = key >> 16
        top = lax.shift_left(jnp.where(hi >= 0, hi, hi ^ 0x7FFF), 16)
        v16_sc[kb] = lax.bitcast_convert_type(top, F32).astype(BF16)
        return carry

    lax.fori_loop(0, nvis, index_block, 0)

    kf = float(k_sel)
    zero_i = jnp.zeros((1, tq), I32)

    def count_ge(cand):
        def body(kb, acc):
            hit = jnp.where(key_sc[kb] >= cand, 1.0, 0.0)
            return acc + jnp.sum(hit.reshape(tk // 8, 8, tq), axis=0)
        acc = lax.fori_loop(0, nvis, body, jnp.zeros((8, tq), F32))
        return jnp.sum(acc, axis=0, keepdims=True)

    def count16_ge(hi):
        hi = jnp.maximum(hi, KEY_NEG_INF >> 16)
        hi = jnp.where((hi > 0) & (hi < 128), 128, jnp.where((hi < 0) & (hi >= -128), 0, hi))
        top = lax.shift_left(jnp.where(hi >= 0, hi, hi ^ 0x7FFF), 16)
        cand = lax.bitcast_convert_type(top, F32).astype(BF16)
        one = jnp.ones((tk, tq), BF16)
        zero = jnp.zeros((tk, tq), BF16)

        def body(kb, acc):
            hit = jnp.where(v16_sc[kb] >= cand, one, zero)
            for r in range(tk // 16):
                acc = acc + hit[r * 16:(r + 1) * 16]
            return acc
        acc = lax.fori_loop(0, nvis, body, jnp.zeros((16, tq), BF16))
        return jnp.sum(acc.astype(F32), axis=0, keepdims=True)

    t16 = jnp.where(count16_ge(zero_i) >= kf, zero_i, jnp.full((1, tq), -32768, I32))

    def bit16_step(i, t):
        cand = t + lax.shift_left(jnp.int32(1), jnp.asarray(14 - i, I32))
        return jnp.where(count16_ge(cand) >= kf, cand, t)

    t16 = lax.fori_loop(0, 15, bit16_step, t16)

    def bit_step(i, t):
        cand = t + lax.shift_left(jnp.int32(1), jnp.asarray(15 - i, I32))
        return jnp.where(count_ge(cand) >= kf, cand, t)

    thr = lax.fori_loop(0, 16, bit_step, lax.shift_left(t16, 16))

    thr_sc[...] = jnp.broadcast_to(thr, thr_sc.shape)
    off_rows = jnp.where(count_ge(thr) >= kf, jnp.where(count_ge(thr + 1) < kf, 0.0, 1.0), 1.0)

    @pl.when(jnp.max(off_rows) > 0.0)
    def _():
        def full_step(i, t):
            cand = t + lax.shift_left(jnp.int32(1), jnp.asarray(30 - i, I32))
            return jnp.where(count_ge(cand) >= kf, cand, t)
        t0 = jnp.where(count_ge(zero_i) >= kf, zero_i, jnp.full((1, tq), INT_MIN, I32))
        thr_sc[...] = jnp.broadcast_to(lax.fori_loop(0, 31, full_step, t0), thr_sc.shape)

    thr = thr_sc[0:1, :]

    need = kf - count_ge(thr + 1)
    thr_eq = jnp.where(thr == KEY_NEG_INF, INT_MIN, thr)

    for h in range(H_B):
        qaug_sc[h] = jnp.concatenate([qb_ref[(h // 2) * LANES:(h // 2 + 1) * LANES, :], augq_ref[h]], axis=0)
    m_sc[...] = jnp.full(m_sc.shape, NEG, F32)
    acc_sc[...] = jnp.zeros(acc_sc.shape, F32)

    def attend(kb, diag, ties_before):
        off = pl.multiple_of(kb * tk, tk)
        kk = kb_ref[pl.ds(off, tk), :]
        zero = jnp.zeros_like(kk)
        rel = qblk * tq - kb * tk
        augk = _aug_key(tk, rel.astype(F32))
        k_aug = (jnp.concatenate([jnp.where(lane_k < DH_B, kk, zero), augk], axis=1),
                 jnp.concatenate([jnp.where(lane_k >= DH_B, kk, zero), augk], axis=1))
        vt = _with_ones_rows(vt_ref[kb])
        key = key_sc[kb]
        tie = key == thr_eq
        tie_rank = _dot(tri_ref[...], jnp.where(tie, 1.0, 0.0).astype(BF16)) + ties_before
        mask = jnp.where(key > thr, 0.0, jnp.where(tie, jnp.where(tie_rank <= need, 0.0, NEG), NEG))
        if diag:
            ahead = jnp.maximum(jj - ii - rel, 0).astype(F32)
        for h in range(H_B):
            st = _dot(k_aug[h % 2], qaug_sc[h]) + mask
            if diag:
                st = st - (2.0 * _sigma(sig_ref, h)) * ahead
            s_sc[h] = st
        for h in range(H_B):
            _softmax_step(h, s_sc[h], vt, m_sc, acc_sc)
        return tie_rank[tk - 1:tk, :]

    ties = lax.fori_loop(0, n_before, lambda kb, t: attend(kb, False, t), jnp.zeros((1, tq), F32))
    for c in range(max(1, tq // tk)):
        ties = attend(n_before + c, True, ties)

    yt = jnp.concatenate([acc_sc[h][:DH_B] / acc_sc[h][DH_B:DH_B + 1] for h in range(H_B)], axis=0)
    o_ref[...] = jnp.transpose(yt).astype(o_ref.dtype)


def _dsa_attention(qbt, kb2, vbt, qit, kik, wit, sig, k_sel, tq, tk):
    b, s, _ = kb2.shape
    w = H_B * DH_B
    assert tq % tk == 0 or tk % tq == 0
    kern = functools.partial(_dsa_kernel, tq=tq, tk=tk, k_sel=k_sel)
    tri = (jnp.arange(tk)[None, :] <= jnp.arange(tk)[:, None]).astype(BF16)
    tile = lambda c: pl.BlockSpec((None, tq, c), lambda bi, i: (bi, i, 0))
    tile_t = lambda r: pl.BlockSpec((None, None, r, tq), lambda bi, i: (bi, i, 0, 0))
    full = lambda c: pl.BlockSpec((None, s, c), lambda bi, i: (bi, 0, 0))
    nq = s // tq
    return pl.pallas_call(
        kern,
        grid=(b, nq),
        in_specs=[pl.BlockSpec(memory_space=pltpu.SMEM),
                  tile_t(w), full(LANES),
                  pl.BlockSpec((None, s // tk, DH_B, tk), lambda bi, i: (bi, 0, 0, 0)),
                  tile_t(H_I * D_I), full(LANES),
                  pl.BlockSpec((WIT_ROWS, tq), lambda bi, i: (0, bi * nq + i)),
                  pl.BlockSpec((H_B, LANES, tq), lambda bi, i: (0, 0, 0)),
                  pl.BlockSpec((tk, tk), lambda bi, i: (0, 0))],
        out_specs=tile(w),
        out_shape=jax.ShapeDtypeStruct((b, s, w), BF16),
        scratch_shapes=[pltpu.VMEM((H_B, 2 * LANES, tq), BF16),
                        pltpu.VMEM((s // tk, tk, tq), I32),
                        pltpu.VMEM((s // tk, tk, tq), BF16),
                        pltpu.VMEM((8, tq), I32),
                        pltpu.VMEM((H_B, tk, tq), F32),
                        pltpu.VMEM((H_B, 8, tq), F32),
                        pltpu.VMEM((H_B, DH_B + SUM_ROWS, tq), F32)],
        compiler_params=pltpu.CompilerParams(dimension_semantics=("arbitrary",) * 2,
                                             vmem_limit_bytes=VMEM_LIMIT),
        name="dsa_attn",
    )(sig, qbt, kb2, vbt, qit, kik, wit, _aug_queries(sig, H_B, tq), tri)


def _memkv_kernel(m_ref, g_ref, w_ref, o_ref):
    mn = _rms(m_ref[...], g_ref[...]).astype(BF16)
    o_ref[...] = _dot(mn, w_ref[...]).astype(o_ref.dtype)


def _memkv(mem2, gain, w_ckv, tm):
    n, d = mem2.shape
    return pl.pallas_call(
        _memkv_kernel,
        grid=(n // tm,),
        in_specs=[pl.BlockSpec((tm, d), lambda i: (i, 0)),
                  pl.BlockSpec((1, d), lambda i: (0, 0)),
                  pl.BlockSpec(w_ckv.shape, lambda i: (0, 0))],
        out_specs=pl.BlockSpec((tm, w_ckv.shape[1]), lambda i: (i, 0)),
        out_shape=jax.ShapeDtypeStruct((n, w_ckv.shape[1]), BF16),
        compiler_params=pltpu.CompilerParams(dimension_semantics=("arbitrary",),
                                             vmem_limit_bytes=VMEM_LIMIT),
        name="memkv",
    )(mem2, gain, w_ckv)


def _merge_kernel(x_ref, ya_ref, yb_ref, mkv_ref, mixg_ref, wgate_ref, bgate_ref, wa_ref, wb_ref, wout_ref,
                  crossg_ref, wcq_ref, wco_ref, ffng_ref, wrh_ref, wrl_ref, h2_ref, lg_ref):
    d = x_ref.shape[-1]
    dh = d // H_X
    x = x_ref[...]
    xn = _rms(x, mixg_ref[...]).astype(BF16)
    gates = jax.nn.sigmoid(_dot(xn, wgate_ref[...]) + bgate_ref[...])
    merged = gates[:, :d] * _dot(ya_ref[...], wa_ref[...]) + gates[:, d:] * _dot(yb_ref[...], wb_ref[...])
    h1 = x + _dot(merged.astype(BF16), wout_ref[...])

    q = _dot(_rms(h1, crossg_ref[...]).astype(BF16), wcq_ref[...]).astype(BF16)
    heads = []
    for h in range(H_X):
        k = mkv_ref[:, h * dh:(h + 1) * dh]
        v = mkv_ref[:, d + h * dh:d + (h + 1) * dh]
        s = _dot_nt(q[:, h * dh:(h + 1) * dh], k)
        e = jnp.exp(s - jnp.max(s, axis=1, keepdims=True))
        p = e / jnp.sum(e, axis=1, keepdims=True)
        heads.append(_dot(p.astype(BF16), v).astype(BF16))
    h2 = h1 + _dot(jnp.concatenate(heads, axis=1), wco_ref[...])
    h2_ref[...] = h2

    f = _rms(h2, ffng_ref[...])
    f_hi = f.astype(BF16)
    f_lo = (f - f_hi.astype(F32)).astype(BF16)
    w_hi = wrh_ref[...]
    lg_ref[...] = _dot_nt(w_hi, f_hi) + _dot_nt(w_hi, f_lo) + _dot_nt(wrl_ref[...], f_hi)


def _merge(x3, ya, yb, mkv, p, tm):
    b, s, d = x3.shape
    m = mkv.shape[1]
    tok = lambda w: pl.BlockSpec((None, tm, w), lambda bi, i: (bi, i, 0))
    const = lambda a: pl.BlockSpec(a.shape, lambda bi, i: (0,) * a.ndim, pipeline_mode=pl.Buffered(1))
    consts = (p["mix_g"], p["w_gate"], p["b_gate"], p["w_a"], p["w_b"], p["w_out"], p["cross_g"],
              p["w_cq"], p["w_co"], p["ffn_g"], p["w_r_hi"], p["w_r_lo"])
    return pl.pallas_call(
        _merge_kernel,
        grid=(b, s // tm),
        in_specs=[tok(d), tok(ya.shape[-1]), tok(yb.shape[-1]),
                  pl.BlockSpec((None, m, 2 * d), lambda bi, i: (bi, 0, 0))] + [const(a) for a in consts],
        out_specs=[tok(d), pl.BlockSpec((LANES, tm), lambda bi, i: (0, bi * (s // tm) + i))],
        out_shape=[jax.ShapeDtypeStruct((b, s, d), F32), jax.ShapeDtypeStruct((LANES, b * s), F32)],
        compiler_params=pltpu.CompilerParams(dimension_semantics=("arbitrary",) * 2,
                                             vmem_limit_bytes=VMEM_LIMIT),
        name="merge",
    )(x3, ya, yb, mkv, *consts)


def _route_kernel(lg_ref, bias_ref, mi_ref, mf_ref, cnt_ref, carry_sc, *, tm):
    @pl.when(pl.program_id(0) == 0)
    def _():
        carry_sc[...] = jnp.zeros(carry_sc.shape, F32)

    lg = lg_ref[...] + bias_ref[...]
    e = lg[0:N_EXPERTS]
    g = lg[N_EXPERTS:N_EXPERTS + 8]
    row_g = lax.broadcasted_iota(I32, (8, tm), 0)
    g = jnp.where(row_g < N_GROUPS, g, -jnp.inf)
    gmax = jnp.max(g, axis=0, keepdims=True)
    g_idx = jnp.min(jnp.where(g == gmax, row_g, N_GROUPS), axis=0, keepdims=True)
    p_g = 1.0 / jnp.sum(jnp.exp(g - gmax), axis=0, keepdims=True)

    row_e = lax.broadcasted_iota(I32, (N_EXPERTS, tm), 0)
    in_grp = (row_e >> 3) == g_idx
    emax = jnp.max(jnp.where(in_grp, e, -jnp.inf), axis=0, keepdims=True)
    ex = jnp.where(in_grp, jnp.exp(jnp.where(in_grp, e - emax, 0.0)), 0.0)
    probs = ex / jnp.sum(ex, axis=0, keepdims=True)
    big = N_EXPERTS
    p1 = jnp.max(probs, axis=0, keepdims=True)
    i1 = jnp.min(jnp.where(in_grp, jnp.where(probs == p1, row_e, big), big), axis=0, keepdims=True)
    probs2 = jnp.where(in_grp, jnp.where(row_e == i1, -1.0, probs), -1.0)
    p2 = jnp.max(probs2, axis=0, keepdims=True)
    i2 = jnp.min(jnp.where(probs2 == p2, row_e, big), axis=0, keepdims=True)
    denom = p1 + p2
    gate1 = p_g * p1 / denom
    gate2 = p_g * p2 / denom

    used = jnp.where(row_e == i1, 1.0, jnp.where(row_e == i2, 1.0, 0.0))
    r = lax.broadcasted_iota(I32, (tm, tm), 0)
    c = lax.broadcasted_iota(I32, (tm, tm), 1)
    before = jnp.where(r < c, 1.0, 0.0).astype(BF16)
    excl = _dot(used.astype(BF16), before) + carry_sc[:, 0:1]
    rank1 = jnp.sum(jnp.where(row_e == i1, excl, 0.0), axis=0, keepdims=True)
    rank2 = jnp.sum(jnp.where(row_e == i2, excl, 0.0), axis=0, keepdims=True)
    carry_sc[...] = carry_sc[...] + jnp.sum(used, axis=1, keepdims=True)
    cnt_ref[...] = carry_sc[...]

    mi_ref[...] = jnp.zeros(mi_ref.shape, I32)
    mi_ref[0:1, :] = i1
    mi_ref[1:2, :] = i2
    mi_ref[2:3, :] = rank1.astype(I32)
    mi_ref[3:4, :] = rank2.astype(I32)
    mf_ref[...] = jnp.zeros(mf_ref.shape, F32)
    mf_ref[0:1, :] = gate1
    mf_ref[1:2, :] = gate2


def _route(lg_t, bias, tm):
    n = lg_t.shape[1]
    kern = functools.partial(_route_kernel, tm=tm)
    return pl.pallas_call(
        kern,
        grid=(n // tm,),
        in_specs=[pl.BlockSpec((LANES, tm), lambda i: (0, i)),
                  pl.BlockSpec((LANES, 1), lambda i: (0, 0))],
        out_specs=[pl.BlockSpec((8, tm), lambda i: (0, i)),
                   pl.BlockSpec((8, tm), lambda i: (0, i)),
                   pl.BlockSpec((N_EXPERTS, LANES), lambda i: (0, 0))],
        out_shape=[jax.ShapeDtypeStruct((8, n), I32), jax.ShapeDtypeStruct((8, n), F32),
                   jax.ShapeDtypeStruct((N_EXPERTS, LANES), F32)],
        scratch_shapes=[pltpu.VMEM((N_EXPERTS, LANES), F32)],
        compiler_params=pltpu.CompilerParams(dimension_semantics=("arbitrary",),
                                             vmem_limit_bytes=VMEM_LIMIT),
        name="route",
    )(lg_t, bias)


ROW_DMA_UNROLL = 8


def _dispatch_kernel(dest_ref, h2_ref, xs_in_hbm, xs_hbm, sem, *, tm):
    del xs_in_hbm

    def row_copy(t0, u, dst_row):
        src = h2_ref.at[pl.ds(t0, ROW_DMA_UNROLL)].at[pl.ds(u, 1)]
        return pltpu.make_async_copy(src, xs_hbm.at[pl.ds(dst_row, 1)], sem)

    def issue(c, carry):
        t0 = pl.multiple_of(c * ROW_DMA_UNROLL, ROW_DMA_UNROLL)
        for u in range(ROW_DMA_UNROLL):
            for j in range(2):
                row_copy(t0, u, dest_ref[0, j * tm + t0 + u]).start()
        return carry

    def drain(t, carry):
        for j in range(2):
            row_copy(0, 0, 0).wait()
        return carry

    lax.fori_loop(0, tm // ROW_DMA_UNROLL, issue, 0)
    lax.fori_loop(0, tm, drain, 0, unroll=ROW_DMA_UNROLL)


def _dispatch(dest, h2, n_slots, tm):
    n, d = h2.shape
    kern = functools.partial(_dispatch_kernel, tm=tm)
    xs0 = jnp.zeros((n_slots, d), F32)
    return pl.pallas_call(
        kern,
        grid=(n // tm,),
        in_specs=[pl.BlockSpec((None, 1, 2 * tm), lambda i: (i, 0, 0), memory_space=pltpu.SMEM),
                  pl.BlockSpec((tm, d), lambda i: (i, 0)),
                  pl.BlockSpec(memory_space=pl.ANY)],
        out_specs=pl.BlockSpec(memory_space=pl.ANY),
        out_shape=jax.ShapeDtypeStruct((n_slots, d), F32),
        scratch_shapes=[pltpu.SemaphoreType.DMA(())],
        input_output_aliases={2: 0},
        compiler_params=pltpu.CompilerParams(dimension_semantics=("arbitrary",),
                                             has_side_effects=True, vmem_limit_bytes=VMEM_LIMIT),
        name="dispatch",
    )(dest, h2, xs0)


def _expert_kernel(be_ref, nu_ref, x_ref, g_ref, w1_ref, w3_ref, w2_ref, y_ref):
    del be_ref

    @pl.when(pl.program_id(0) < nu_ref[0])
    def _():
        xb = _rms(x_ref[...], g_ref[...]).astype(BF16)
        a = _dot(xb, w1_ref[...])
        hb = (a * jax.nn.sigmoid(a)) * _dot(xb, w3_ref[...])
        y_ref[...] = _dot(hb.astype(BF16), w2_ref[...])

    @pl.when(pl.program_id(0) >= nu_ref[0])
    def _():
        y_ref[...] = jnp.zeros(y_ref.shape, y_ref.dtype)


def _experts(block_expert, n_used, xs, gain, w1, w3, w2, tb):
    n_slots, d = xs.shape
    de = w1.shape[-1]
    nb = n_slots // tb
    row = lambda i, be, nu: (i, 0)
    grid_spec = pltpu.PrefetchScalarGridSpec(
        num_scalar_prefetch=2,
        grid=(nb,),
        in_specs=[pl.BlockSpec((tb, d), row),
                  pl.BlockSpec((1, d), lambda i, be, nu: (0, 0)),
                  pl.BlockSpec((None, d, de), lambda i, be, nu: (be[i], 0, 0)),
                  pl.BlockSpec((None, d, de), lambda i, be, nu: (be[i], 0, 0)),
                  pl.BlockSpec((None, de, d), lambda i, be, nu: (be[i], 0, 0))],
        out_specs=pl.BlockSpec((tb, d), row),
    )
    return pl.pallas_call(
        _expert_kernel,
        grid_spec=grid_spec,
        out_shape=jax.ShapeDtypeStruct((n_slots, d), F32),
        compiler_params=pltpu.CompilerParams(dimension_semantics=("arbitrary",),
                                             vmem_limit_bytes=VMEM_LIMIT),
        name="experts",
    )(block_expert, n_used, xs, gain, w1, w3, w2)


def _combine_kernel(dest_ref, mf_ref, h2_ref, g_ref, y_hbm, o_ref, ybuf, sem, *, tm):
    def row_copy(src_row, j, t0, u):
        dst = ybuf.at[j, pl.ds(t0, ROW_DMA_UNROLL)].at[pl.ds(u, 1)]
        return pltpu.make_async_copy(y_hbm.at[pl.ds(src_row, 1)], dst, sem)

    def issue(c, carry):
        t0 = pl.multiple_of(c * ROW_DMA_UNROLL, ROW_DMA_UNROLL)
        for u in range(ROW_DMA_UNROLL):
            for j in range(2):
                row_copy(dest_ref[0, j * tm + t0 + u], j, t0, u).start()
        return carry

    def drain(t, carry):
        for j in range(2):
            row_copy(0, j, 0, 0).wait()
        return carry

    lax.fori_loop(0, tm // ROW_DMA_UNROLL, issue, 0)
    lax.fori_loop(0, tm, drain, 0, unroll=ROW_DMA_UNROLL)

    gates = jnp.concatenate([mf_ref[...], jnp.zeros((LANES - 8, tm), F32)], axis=0)
    gt = jnp.transpose(gates)
    h = h2_ref[...] + gt[:, 0:1] * ybuf[0] + gt[:, 1:2] * ybuf[1]
    o_ref[...] = _rms(h, g_ref[...])


def _combine(dest, mf, h2, gain, y, tm):
    n, d = h2.shape
    kern = functools.partial(_combine_kernel, tm=tm)
    return pl.pallas_call(
        kern,
        grid=(n // tm,),
        in_specs=[pl.BlockSpec((None, 1, 2 * tm), lambda i: (i, 0, 0), memory_space=pltpu.SMEM),
                  pl.BlockSpec((8, tm), lambda i: (0, i)),
                  pl.BlockSpec((tm, d), lambda i: (i, 0)),
                  pl.BlockSpec((1, d), lambda i: (0, 0)),
                  pl.BlockSpec(memory_space=pl.ANY)],
        out_specs=pl.BlockSpec((tm, d), lambda i: (i, 0)),
        out_shape=jax.ShapeDtypeStruct((n, d), F32),
        scratch_shapes=[pltpu.VMEM((2, tm, d), F32), pltpu.SemaphoreType.DMA(())],
        compiler_params=pltpu.CompilerParams(dimension_semantics=("arbitrary",),
                                             vmem_limit_bytes=VMEM_LIMIT),
        name="combine",
    )(dest, mf, h2, gain, y)


def _sigma_parts(n_heads):
    sigma = jnp.asarray([2.0 ** (-8.0 * (i + 1) / n_heads) for i in range(n_heads)], dtype=F32) * LOG2E
    s1 = sigma.astype(BF16).astype(F32)
    s2 = (sigma - s1).astype(BF16).astype(F32)
    s3 = ((sigma - s1) - s2).astype(BF16).astype(F32)
    return jnp.stack([s1, s2, s3], axis=1).reshape(-1)


def _prep_w_in(w_in):
    splits = []
    acc = 0
    for w in IN_WIDTHS[:-1]:
        acc += w
        splits.append(acc)
    qa, ka, va, qb, kb, vb, qi, ki, wi = jnp.split(w_in, splits, axis=1)
    w_n = jnp.concatenate([ka, kb, kb, ki, ki], axis=1)
    pad = jnp.zeros((w_in.shape[0], WIT_ROWS - IN_WIDTHS[-1]), w_in.dtype)
    w_t = jnp.concatenate([qa * (DH_A ** -0.5 * LOG2E), qb * (DH_B ** -0.5 * LOG2E), qi, va, vb,
                           wi * (H_I ** -0.5 * D_I ** -0.5), pad], axis=1).T
    return w_n.astype(BF16), w_t.astype(BF16)


def _prep_router(w_group, w_router):
    d = w_group.shape[0]
    w = jnp.concatenate([w_router, w_group, jnp.zeros((d, LANES - N_EXPERTS - N_GROUPS), F32)], axis=1).T
    hi = w.astype(BF16)
    lo = (w - hi.astype(F32)).astype(BF16)
    return hi, lo


def _block_sizes(s):
    return dict(tm_proj=512, tq=256, tk=TK, tm_merge=min(512, s), tm_route=512, tm_disp=512, tb=512, tm_comb=256)


def kernel(x, mem, mix_norm, w_in, lam_q1, lam_k1, lam_q2, lam_k2, diff_subln, w_branch_a, w_branch_b, w_gate,
           b_gate, w_out, cross_norm, mem_norm, w_cq, w_ckv, w_co, ffn_norm, w_group, b_group, w_router,
           b_router, w1, w3, w2, final_norm):
    b, s, d = x.shape
    n = b * s
    m = mem.shape[1]
    bs = _block_sizes(s)
    k_sel = min(TOPK_MAX, s // 4)
    row = lambda v: v.reshape(1, -1).astype(F32)

    w_n, w_t = _prep_w_in(w_in[0])
    ka, kb2, kik, qat, qbt, qit, vat, vbt, wit = _proj(x.reshape(n, d), row(mix_norm[0]), w_n, w_t, bs["tm_proj"])
    ka, kb2, kik = [o.reshape(b, s, -1) for o in (ka, kb2, kik)]
    qat, qbt, qit, vat, vbt = [o.reshape(b, s // TK, -1, TK) for o in (qat, qbt, qit, vat, vbt)]
    lam = (jnp.exp(jnp.sum(lam_q1[0].astype(F32) * lam_k1[0].astype(F32)))
           - jnp.exp(jnp.sum(lam_q2[0].astype(F32) * lam_k2[0].astype(F32))) + LAMBDA_INIT).reshape(1)
    subln_t = jnp.broadcast_to(diff_subln[0].astype(F32)[:, None], (LANES, LANES))
    ya = _diff_attention(qat, ka, vat, _sigma_parts(H_A), lam, subln_t, bs["tq"], bs["tk"])
    yb = _dsa_attention(qbt, kb2, vbt, qit, kik, wit, _sigma_parts(H_B), k_sel, bs["tq"], bs["tk"])

    mkv = _memkv(mem.reshape(b * m, d), row(mem_norm[0]), w_ckv[0].astype(BF16), min(512, b * m))
    w_r_hi, w_r_lo = _prep_router(w_group[0], w_router[0])
    params = dict(mix_g=row(mix_norm[0]), w_gate=w_gate[0].astype(BF16), b_gate=row(b_gate[0]),
                  w_a=w_branch_a[0].astype(BF16), w_b=w_branch_b[0].astype(BF16), w_out=w_out[0].astype(BF16),
                  cross_g=row(cross_norm[0]), w_cq=(w_cq[0] * (d // H_X) ** -0.5).astype(BF16),
                  w_co=w_co[0].astype(BF16), ffn_g=row(ffn_norm[0]), w_r_hi=w_r_hi, w_r_lo=w_r_lo)
    h2, lg_t = _merge(x, ya, yb, mkv.reshape(b, m, 2 * d), params, bs["tm_merge"])
    h2 = h2.reshape(n, d)

    bias = jnp.concatenate([b_router[0], b_group[0], jnp.zeros((LANES - N_EXPERTS - N_GROUPS,), F32)])
    mi, mf, cnt = _route(lg_t, bias.reshape(LANES, 1).astype(F32), bs["tm_route"])
    tb = bs["tb"]
    counts = cnt[:, 0].astype(I32)
    padded = (counts + tb - 1) // tb * tb
    pad_end = jnp.cumsum(padded)
    pad_start = (pad_end - padded).astype(I32)
    n_blocks = (2 * n) // tb + N_EXPERTS
    blk_first = jnp.arange(n_blocks, dtype=I32) * tb
    block_expert = jnp.minimum(jnp.sum((pad_end[None, :] <= blk_first[:, None]).astype(I32), axis=1), N_EXPERTS - 1)
    n_used = (pad_end[-1] // tb).astype(I32).reshape(1)
    start_of = jnp.sum(jnp.where(mi[0:2][None] == jnp.arange(N_EXPERTS, dtype=I32)[:, None, None],
                                 pad_start[:, None, None], 0), axis=0)
    dest = start_of + mi[2:4]

    def per_tile(tm):
        return dest.reshape(2, n // tm, tm).transpose(1, 0, 2).reshape(n // tm, 1, 2 * tm)

    xs = _dispatch(per_tile(bs["tm_disp"]), h2, n_blocks * tb, bs["tm_disp"])
    y = _experts(block_expert, n_used, xs, row(ffn_norm[0]), w1[0].astype(BF16), w3[0].astype(BF16),
                 w2[0].astype(BF16), tb)
    out = _combine(per_tile(bs["tm_comb"]), mf, h2, row(final_norm), y, bs["tm_comb"])
    return out.reshape(b, s, d)
```

```python
import functools

import jax
import jax.numpy as jnp
from jax import lax
from jax.experimental import pallas as pl
from jax.experimental.pallas import tpu as pltpu

F32 = jnp.float32
BF16 = jnp.bfloat16
I32 = jnp.int32

EPS = 1e-6
CHUNK_SHIFT = 6
H_A, DH_A = 4, 64
H_B, DH_B = 8, 64
H_I, D_I = 4, 64
TOPK_MAX = 256
H_X = 4
N_GROUPS, EXP_PER_GROUP = 4, 8
N_EXPERTS = N_GROUPS * EXP_PER_GROUP
LAMBDA_INIT = 0.8 - 0.6 * 1.0
LANES = 128
NEG = -1e30
INT_MIN = -2147483648
KEY_NEG_INF = -2139095041
VMEM_LIMIT = 56 * 1024 * 1024

LOG2E = 1.4426950408889634
TK = 256
WIT_ROWS = 16
SUM_ROWS = 16
IN_WIDTHS = (512, 512, 512, 512, 64, 64, 256, 64, 4)


def _rms(x, g):
    ms = jnp.mean(x * x, axis=-1, keepdims=True)
    return (x * lax.rsqrt(ms + EPS)) * g


def _dot_nt(a, b):
    return lax.dot_general(a, b, (((1,), (1,)), ((), ())), preferred_element_type=F32)


def _dot(a, b):
    return jnp.dot(a, b, preferred_element_type=F32)


def _proj_kernel(x_ref, g_ref, wn_ref, wt_ref, ka_ref, kb_ref, kik_ref,
                 qat_ref, qbt_ref, qit_ref, vat_ref, vbt_ref, wit_ref):
    xn = _rms(x_ref[...], g_ref[...]).astype(BF16)
    col = 0
    for ref in (ka_ref, kb_ref, kik_ref):
        n = ref.shape[-1]
        ref[...] = _dot(xn, wn_ref[:, col:col + n]).astype(ref.dtype)
        col += n
    row = 0
    for ref in (qat_ref, qbt_ref, qit_ref, vat_ref, vbt_ref):
        r = ref.shape[1]
        yt = _dot_nt(wt_ref[row:row + r, :], xn)
        for c in range(ref.shape[0]):
            ref[c] = yt[:, c * TK:(c + 1) * TK].astype(ref.dtype)
        row += r
    wit_ref[...] = _dot_nt(wt_ref[row:row + wit_ref.shape[0], :], xn)


def _proj(x2, gain, w_n, w_t, tm):
    n, d = x2.shape
    kb = tm // TK
    tok = lambda w: pl.BlockSpec((tm, w), lambda i: (i, 0))
    widths = (H_A * 2 * DH_A, LANES, LANES)
    rows_t = (H_A * 2 * DH_A, H_B * DH_B, H_I * D_I, H_A * 2 * DH_A, DH_B)
    return pl.pallas_call(
        _proj_kernel,
        grid=(n // tm,),
        in_specs=[pl.BlockSpec((tm, d), lambda i: (i, 0)),
                  pl.BlockSpec((1, d), lambda i: (0, 0)),
                  pl.BlockSpec(w_n.shape, lambda i: (0, 0)),
                  pl.BlockSpec(w_t.shape, lambda i: (0, 0))],
        out_specs=[tok(w) for w in widths] + [pl.BlockSpec((kb, r, TK), lambda i: (i, 0, 0)) for r in rows_t] + [
            pl.BlockSpec((WIT_ROWS, tm), lambda i: (0, i))],
        out_shape=[jax.ShapeDtypeStruct((n, w), BF16) for w in widths] + [
            jax.ShapeDtypeStruct((n // TK, r, TK), BF16) for r in rows_t] + [
            jax.ShapeDtypeStruct((WIT_ROWS, n), F32)],
        compiler_params=pltpu.CompilerParams(dimension_semantics=("arbitrary",),
                                             vmem_limit_bytes=VMEM_LIMIT),
        name="proj",
    )(x2, gain, w_n, w_t)


def _aug_query(sig_ref, h, tq):
    lane = lax.broadcasted_iota(I32, (tq, LANES), 1)
    i = lax.broadcasted_iota(I32, (tq, LANES), 0).astype(F32)
    s1, s2, s3 = sig_ref[3 * h], sig_ref[3 * h + 1], sig_ref[3 * h + 2]
    c = ((s1 + s2) + s3) * i
    c1 = c.astype(BF16).astype(F32)
    c2 = (c - c1).astype(BF16).astype(F32)
    c3 = (c - c1) - c2
    out = jnp.zeros((tq, LANES), F32)
    for n, v in enumerate((c1, c2, c3, s1, s2, s3, s1, s2, s3, -s1, -s2, -s3)):
        out = jnp.where(lane == n, v, out)
    return out.astype(BF16)


def _aug_queries(sig, n_heads, tq):
    return jnp.stack([_aug_query(sig, h, tq).T for h in range(n_heads)])


def _aug_key(tk, off):
    lane = lax.broadcasted_iota(I32, (tk, LANES), 1)
    j = lax.broadcasted_iota(I32, (tk, LANES), 0)
    j_lo = (j & 255).astype(F32)
    j_hi = (j - (j & 255)).astype(F32)
    base = jnp.where(lane < 3, -1.0, jnp.where(lane < 6, j_lo, jnp.where(lane < 9, j_hi, jnp.where(lane < 12, off, 0.0))))
    return base.astype(BF16)


def _sigma(sig_ref, h):
    return (sig_ref[3 * h] + sig_ref[3 * h + 1]) + sig_ref[3 * h + 2]


def _with_ones_rows(vt):
    return jnp.concatenate([vt, jnp.ones((SUM_ROWS, vt.shape[1]), vt.dtype)], axis=0)


def _softmax_step(r, st, vt_ones, m_sc, acc_sc):
    m_prev = m_sc[r][0:1, :]
    m_next = jnp.maximum(m_prev, jnp.max(st, axis=0, keepdims=True))
    p = jnp.exp2(st - m_next)
    alpha = jnp.exp2(m_prev - m_next)
    acc_sc[r] = alpha * acc_sc[r] + _dot(vt_ones, p.astype(BF16))
    m_sc[r] = jnp.broadcast_to(m_next, m_sc.shape[1:])


def _diff_kernel(sig_ref, lam_ref, q_ref, k_ref, vt_ref, g_ref, augq_ref, o_ref,
                 qaug_sc, s_sc, m_sc, acc_sc, *, tq, tk):
    qblk = pl.program_id(1)
    lam = lam_ref[0]
    lane_k = lax.broadcasted_iota(I32, (tk, LANES), 1)
    for h in range(H_A):
        qaug_sc[h] = jnp.concatenate([q_ref[h * LANES:(h + 1) * LANES, :], augq_ref[h]], axis=0)
    m_sc[...] = jnp.full(m_sc.shape, NEG, F32)
    acc_sc[...] = jnp.zeros(acc_sc.shape, F32)

    def block(kb, diag):
        off = pl.multiple_of(kb * tk, tk)
        k = k_ref[pl.ds(off, tk), :]
        rel = qblk * tq - kb * tk
        augk = _aug_key(tk, rel.astype(F32))
        if diag:
            jj = lax.broadcasted_iota(I32, (tk, tq), 0)
            ii = lax.broadcasted_iota(I32, (tk, tq), 1)
            allowed = ((off + jj) >> CHUNK_SHIFT) <= ((qblk * tq + ii) >> CHUNK_SHIFT)
            ahead = jnp.maximum(jj - ii - rel, 0).astype(F32)
        for h in range(H_A):
            kh = k[:, h * LANES:(h + 1) * LANES]
            zero = jnp.zeros_like(kh)
            qa = qaug_sc[h]
            for m in range(2):
                km = jnp.where(lane_k < DH_A if m == 0 else lane_k >= DH_A, kh, zero)
                st = _dot(jnp.concatenate([km, augk], axis=1), qa)
                if diag:
                    st = jnp.where(allowed, st - (2.0 * _sigma(sig_ref, h)) * ahead, NEG)
                s_sc[2 * h + m] = st
        for h in range(H_A):
            vt = _with_ones_rows(vt_ref[kb, h * LANES:(h + 1) * LANES, :])
            for m in range(2):
                _softmax_step(2 * h + m, s_sc[2 * h + m], vt, m_sc, acc_sc)

    def off_diagonal(kb, carry):
        block(kb, False)
        return carry

    n_before = (qblk * tq) // tk
    lax.fori_loop(0, n_before, off_diagonal, 0)
    for c in range(max(1, tq // tk)):
        block(n_before + c, True)

    gain = jnp.concatenate([g_ref[...]] * (tq // LANES), axis=1)
    for h in range(H_A):
        a0, a1 = acc_sc[2 * h], acc_sc[2 * h + 1]
        y = a0[:LANES] / a0[LANES:LANES + 1] - lam * (a1[:LANES] / a1[LANES:LANES + 1])
        ms = jnp.mean(y * y, axis=0, keepdims=True)
        yn = (y * lax.rsqrt(ms + EPS)) * gain
        o_ref[:, h * LANES:(h + 1) * LANES] = (jnp.transpose(yn) * (1.0 - LAMBDA_INIT)).astype(o_ref.dtype)


def _diff_attention(qat, ka, vat, sig, lam, subln_t, tq, tk):
    b, s, w = ka.shape
    assert tq % tk == 0 or tk % tq == 0
    kern = functools.partial(_diff_kernel, tq=tq, tk=tk)
    smem = pl.BlockSpec(memory_space=pltpu.SMEM)
    return pl.pallas_call(
        kern,
        grid=(b, s // tq),
        in_specs=[smem, smem,
                  pl.BlockSpec((None, None, w, tq), lambda bi, i: (bi, i, 0, 0)),
                  pl.BlockSpec((None, s, w), lambda bi, i: (bi, 0, 0)),
                  pl.BlockSpec((None, s // tk, w, tk), lambda bi, i: (bi, 0, 0, 0)),
                  pl.BlockSpec((LANES, LANES), lambda bi, i: (0, 0)),
                  pl.BlockSpec((H_A, LANES, tq), lambda bi, i: (0, 0, 0))],
        out_specs=pl.BlockSpec((None, tq, w), lambda bi, i: (bi, i, 0)),
        out_shape=jax.ShapeDtypeStruct((b, s, w), BF16),
        scratch_shapes=[pltpu.VMEM((H_A, 2 * LANES, tq), BF16),
                        pltpu.VMEM((2 * H_A, tk, tq), F32),
                        pltpu.VMEM((2 * H_A, 8, tq), F32),
                        pltpu.VMEM((2 * H_A, LANES + SUM_ROWS, tq), F32)],
        compiler_params=pltpu.CompilerParams(dimension_semantics=("arbitrary",) * 2,
                                             vmem_limit_bytes=VMEM_LIMIT),
        name="diff_attn",
    )(sig, lam, qat, ka, vat, subln_t, _aug_queries(sig, H_A, tq))


def _dsa_kernel(sig_ref, qb_ref, kb_ref, vt_ref, qi_ref, kik_ref, wit_ref, augq_ref, tri_ref, o_ref,
                qaug_sc, key_sc, v16_sc, thr_sc, s_sc, m_sc, acc_sc, *, tq, tk, k_sel):
    qblk = pl.program_id(1)
    n_before = (qblk * tq) // tk
    nvis = n_before + max(1, tq // tk)
    lane_k = lax.broadcasted_iota(I32, (tk, LANES), 1)
    jj = lax.broadcasted_iota(I32, (tk, tq), 0)
    ii = lax.broadcasted_iota(I32, (tk, tq), 1)
    t_chunk = (qblk * tq + ii) >> CHUNK_SHIFT
    w_idx = wit_ref[...]
    qidx = qi_ref[...]

    def index_block(kb, carry):
        off = pl.multiple_of(kb * tk, tk)
        kik = kik_ref[pl.ds(off, tk), :]
        zero = jnp.zeros_like(kik)
        k_half = (jnp.where(lane_k < D_I, kik, zero), jnp.where(lane_k >= D_I, kik, zero))
        isc = jnp.zeros((tk, tq), F32)
        for h in range(H_I):
            d = _dot(k_half[h % 2], qidx[(h // 2) * LANES:(h // 2 + 1) * LANES, :])
            isc = isc + w_idx[h:h + 1, :] * jnp.maximum(d, 0.0)
        bits = lax.bitcast_convert_type(isc, I32)
        key = jnp.where(bits < 0, bits ^ 0x7FFFFFFF, bits)
        key = jnp.where(isc == 0.0, 0, key)
        allowed = ((off + jj) >> CHUNK_SHIFT) <= t_chunk
        key = jnp.where(allowed, key, KEY_NEG_INF)
        key_sc[kb] = key
        hi = key >> 16
        top = lax.shift_left(jnp.where(hi >= 0, hi, hi ^ 0x7FFF), 16)
        v16_sc[kb] = lax.bitcast_convert_type(top, F32).astype(BF16)
        return carry

    lax.fori_loop(0, nvis, index_block, 0)

    kf = float(k_sel)
    zero_i = jnp.zeros((1, tq), I32)

    def count_ge(cand):
        def body(kb, acc):
            hit = jnp.where(key_sc[kb] >= cand, 1.0, 0.0)
            return acc + jnp.sum(hit.reshape(tk // 8, 8, tq), axis=0)
        acc = lax.fori_loop(0, nvis, body, jnp.zeros((8, tq), F32))
        return jnp.sum(acc, axis=0, keepdims=True)

    def count16_ge(hi):
        hi = jnp.maximum(hi, KEY_NEG_INF >> 16)
        hi = jnp.where((hi > 0) & (hi < 128), 128, jnp.where((hi < 0) & (hi >= -128), 0, hi))
        top = lax.shift_left(jnp.where(hi >= 0, hi, hi ^ 0x7FFF), 16)
        cand = lax.bitcast_convert_type(top, F32).astype(BF16)
        one = jnp.ones((tk, tq), BF16)
        zero = jnp.zeros((tk, tq), BF16)

        def body(kb, acc):
            hit = jnp.where(v16_sc[kb] >= cand, one, zero)
            for r in range(tk // 16):
                acc = acc + hit[r * 16:(r + 1) * 16]
            return acc
        acc = lax.fori_loop(0, nvis, body, jnp.zeros((16, tq), BF16))
        return jnp.sum(acc.astype(F32), axis=0, keepdims=True)

    t16 = jnp.where(count16_ge(zero_i) >= kf, zero_i, jnp.full((1, tq), -32768, I32))

    def bit16_step(i, t):
        cand = t + lax.shift_left(jnp.int32(1), jnp.asarray(14 - i, I32))
        return jnp.where(count16_ge(cand) >= kf, cand, t)

    t16 = lax.fori_loop(0, 15, bit16_step, t16)

    def bit_step(i, t):
        cand = t + lax.shift_left(jnp.int32(1), jnp.asarray(15 - i, I32))
        return jnp.where(count_ge(cand) >= kf, cand, t)

    thr = lax.fori_loop(0, 16, bit_step, lax.shift_left(t16, 16))

    thr_sc[...] = jnp.broadcast_to(thr, thr_sc.shape)
    off_rows = jnp.where(count_ge(thr) >= kf, jnp.where(count_ge(thr + 1) < kf, 0.0, 1.0), 1.0)

    @pl.when(jnp.max(off_rows) > 0.0)
    def _():
        def full_step(i, t):
            cand = t + lax.shift_left(jnp.int32(1), jnp.asarray(30 - i, I32))
            return jnp.where(count_ge(cand) >= kf, cand, t)
        t0 = jnp.where(count_ge(zero_i) >= kf, zero_i, jnp.full((1, tq), INT_MIN, I32))
        thr_sc[...] = jnp.broadcast_to(lax.fori_loop(0, 31, full_step, t0), thr_sc.shape)

    thr = thr_sc[0:1, :]

    need = kf - count_ge(thr + 1)
    thr_eq = jnp.where(thr == KEY_NEG_INF, INT_MIN, thr)

    for h in range(H_B):
        qaug_sc[h] = jnp.concatenate([qb_ref[(h // 2) * LANES:(h // 2 + 1) * LANES, :], augq_ref[h]], axis=0)
    m_sc[...] = jnp.full(m_sc.shape, NEG, F32)
    acc_sc[...] = jnp.zeros(acc_sc.shape, F32)

    def attend(kb, diag, ties_before):
        off = pl.multiple_of(kb * tk, tk)
        kk = kb_ref[pl.ds(off, tk), :]
        zero = jnp.zeros_like(kk)
        rel = qblk * tq - kb * tk
        augk = _aug_key(tk, rel.astype(F32))
        k_aug = (jnp.concatenate([jnp.where(lane_k < DH_B, kk, zero), augk], axis=1),
                 jnp.concatenate([jnp.where(lane_k >= DH_B, kk, zero), augk], axis=1))
        vt = _with_ones_rows(vt_ref[kb])
        key = key_sc[kb]
        tie = key == thr_eq
        tie_rank = _dot(tri_ref[...], jnp.where(tie, 1.0, 0.0).astype(BF16)) + ties_before
        mask = jnp.where(key > thr, 0.0, jnp.where(tie, jnp.where(tie_rank <= need, 0.0, NEG), NEG))
        if diag:
            ahead = jnp.maximum(jj - ii - rel, 0).astype(F32)
        for h in range(H_B):
            st = _dot(k_aug[h % 2], qaug_sc[h]) + mask
            if diag:
                st = st - (2.0 * _sigma(sig_ref, h)) * ahead
            s_sc[h] = st
        for h in range(H_B):
            _softmax_step(h, s_sc[h], vt, m_sc, acc_sc)
        return tie_rank[tk - 1:tk, :]

    ties = lax.fori_loop(0, n_before, lambda kb, t: attend(kb, False, t), jnp.zeros((1, tq), F32))
    for c in range(max(1, tq // tk)):
        ties = attend(n_before + c, True, ties)

    yt = jnp.concatenate([acc_sc[h][:DH_B] / acc_sc[h][DH_B:DH_B + 1] for h in range(H_B)], axis=0)
    o_ref[...] = jnp.transpose(yt).astype(o_ref.dtype)


def _dsa_attention(qbt, kb2, vbt, qit, kik, wit, sig, k_sel, tq, tk):
    b, s, _ = kb2.shape
    w = H_B * DH_B
    assert tq % tk == 0 or tk % tq == 0
    kern = functools.partial(_dsa_kernel, tq=tq, tk=tk, k_sel=k_sel)
    tri = (jnp.arange(tk)[None, :] <= jnp.arange(tk)[:, None]).astype(BF16)
    tile = lambda c: pl.BlockSpec((None, tq, c), lambda bi, i: (bi, i, 0))
    tile_t = lambda r: pl.BlockSpec((None, None, r, tq), lambda bi, i: (bi, i, 0, 0))
    full = lambda c: pl.BlockSpec((None, s, c), lambda bi, i: (bi, 0, 0))
    nq = s // tq
    return pl.pallas_call(
        kern,
        grid=(b, nq),
        in_specs=[pl.BlockSpec(memory_space=pltpu.SMEM),
                  tile_t(w), full(LANES),
                  pl.BlockSpec((None, s // tk, DH_B, tk), lambda bi, i: (bi, 0, 0, 0)),
                  tile_t(H_I * D_I), full(LANES),
                  pl.BlockSpec((WIT_ROWS, tq), lambda bi, i: (0, bi * nq + i)),
                  pl.BlockSpec((H_B, LANES, tq), lambda bi, i: (0, 0, 0)),
                  pl.BlockSpec((tk, tk), lambda bi, i: (0, 0))],
        out_specs=tile(w),
        out_shape=jax.ShapeDtypeStruct((b, s, w), BF16),
        scratch_shapes=[pltpu.VMEM((H_B, 2 * LANES, tq), BF16),
                        pltpu.VMEM((s // tk, tk, tq), I32),
                        pltpu.VMEM((s // tk, tk, tq), BF16),
                        pltpu.VMEM((8, tq), I32),
                        pltpu.VMEM((H_B, tk, tq), F32),
                        pltpu.VMEM((H_B, 8, tq), F32),
                        pltpu.VMEM((H_B, DH_B + SUM_ROWS, tq), F32)],
        compiler_params=pltpu.CompilerParams(dimension_semantics=("arbitrary",) * 2,
                                             vmem_limit_bytes=VMEM_LIMIT),
        name="dsa_attn",
    )(sig, qbt, kb2, vbt, qit, kik, wit, _aug_queries(sig, H_B, tq), tri)


def _memkv_kernel(m_ref, g_ref, w_ref, o_ref):
    mn = _rms(m_ref[...], g_ref[...]).astype(BF16)
    o_ref[...] = _dot(mn, w_ref[...]).astype(o_ref.dtype)


def _memkv(mem2, gain, w_ckv, tm):
    n, d = mem2.shape
    return pl.pallas_call(
        _memkv_kernel,
        grid=(n // tm,),
        in_specs=[pl.BlockSpec((tm, d), lambda i: (i, 0)),
                  pl.BlockSpec((1, d), lambda i: (0, 0)),
                  pl.BlockSpec(w_ckv.shape, lambda i: (0, 0))],
        out_specs=pl.BlockSpec((tm, w_ckv.shape[1]), lambda i: (i, 0)),
        out_shape=jax.ShapeDtypeStruct((n, w_ckv.shape[1]), BF16),
        compiler_params=pltpu.CompilerParams(dimension_semantics=("arbitrary",),
                                             vmem_limit_bytes=VMEM_LIMIT),
        name="memkv",
    )(mem2, gain, w_ckv)


def _merge_kernel(x_ref, ya_ref, yb_ref, mkv_ref, mixg_ref, wgate_ref, bgate_ref, wa_ref, wb_ref, wout_ref,
                  crossg_ref, wcq_ref, wco_ref, ffng_ref, wrh_ref, wrl_ref, h2_ref, lg_ref):
    d = x_ref.shape[-1]
    dh = d // H_X
    x = x_ref[...]
    xn = _rms(x, mixg_ref[...]).astype(BF16)
    gates = jax.nn.sigmoid(_dot(xn, wgate_ref[...]) + bgate_ref[...])
    merged = gates[:, :d] * _dot(ya_ref[...], wa_ref[...]) + gates[:, d:] * _dot(yb_ref[...], wb_ref[...])
    h1 = x + _dot(merged.astype(BF16), wout_ref[...])

    q = _dot(_rms(h1, crossg_ref[...]).astype(BF16), wcq_ref[...]).astype(BF16)
    heads = []
    for h in range(H_X):
        k = mkv_ref[:, h * dh:(h + 1) * dh]
        v = mkv_ref[:, d + h * dh:d + (h + 1) * dh]
        s = _dot_nt(q[:, h * dh:(h + 1) * dh], k)
        e = jnp.exp(s - jnp.max(s, axis=1, keepdims=True))
        p = e / jnp.sum(e, axis=1, keepdims=True)
        heads.append(_dot(p.astype(BF16), v).astype(BF16))
    h2 = h1 + _dot(jnp.concatenate(heads, axis=1), wco_ref[...])
    h2_ref[...] = h2

    f = _rms(h2, ffng_ref[...])
    f_hi = f.astype(BF16)
    f_lo = (f - f_hi.astype(F32)).astype(BF16)
    w_hi = wrh_ref[...]
    lg_ref[...] = _dot_nt(w_hi, f_hi) + _dot_nt(w_hi, f_lo) + _dot_nt(wrl_ref[...], f_hi)


def _merge(x3, ya, yb, mkv, p, tm):
    b, s, d = x3.shape
    m = mkv.shape[1]
    tok = lambda w: pl.BlockSpec((None, tm, w), lambda bi, i: (bi, i, 0))
    const = lambda a: pl.BlockSpec(a.shape, lambda bi, i: (0,) * a.ndim, pipeline_mode=pl.Buffered(1))
    consts = (p["mix_g"], p["w_gate"], p["b_gate"], p["w_a"], p["w_b"], p["w_out"], p["cross_g"],
              p["w_cq"], p["w_co"], p["ffn_g"], p["w_r_hi"], p["w_r_lo"])
    return pl.pallas_call(
        _merge_kernel,
        grid=(b, s // tm),
        in_specs=[tok(d), tok(ya.shape[-1]), tok(yb.shape[-1]),
                  pl.BlockSpec((None, m, 2 * d), lambda bi, i: (bi, 0, 0))] + [const(a) for a in consts],
        out_specs=[tok(d), pl.BlockSpec((LANES, tm), lambda bi, i: (0, bi * (s // tm) + i))],
        out_shape=[jax.ShapeDtypeStruct((b, s, d), F32), jax.ShapeDtypeStruct((LANES, b * s), F32)],
        compiler_params=pltpu.CompilerParams(dimension_semantics=("arbitrary",) * 2,
                                             vmem_limit_bytes=VMEM_LIMIT),
        name="merge",
    )(x3, ya, yb, mkv, *consts)


def _route_kernel(lg_ref, bias_ref, mi_ref, mf_ref, cnt_ref, carry_sc, *, tm):
    @pl.when(pl.program_id(0) == 0)
    def _():
        carry_sc[...] = jnp.zeros(carry_sc.shape, F32)

    lg = lg_ref[...] + bias_ref[...]
    e = lg[0:N_EXPERTS]
    g = lg[N_EXPERTS:N_EXPERTS + 8]
    row_g = lax.broadcasted_iota(I32, (8, tm), 0)
    g = jnp.where(row_g < N_GROUPS, g, -jnp.inf)
    gmax = jnp.max(g, axis=0, keepdims=True)
    g_idx = jnp.min(jnp.where(g == gmax, row_g, N_GROUPS), axis=0, keepdims=True)
    p_g = 1.0 / jnp.sum(jnp.exp(g - gmax), axis=0, keepdims=True)

    row_e = lax.broadcasted_iota(I32, (N_EXPERTS, tm), 0)
    in_grp = (row_e >> 3) == g_idx
    emax = jnp.max(jnp.where(in_grp, e, -jnp.inf), axis=0, keepdims=True)
    ex = jnp.where(in_grp, jnp.exp(jnp.where(in_grp, e - emax, 0.0)), 0.0)
    probs = ex / jnp.sum(ex, axis=0, keepdims=True)
    big = N_EXPERTS
    p1 = jnp.max(probs, axis=0, keepdims=True)
    i1 = jnp.min(jnp.where(in_grp, jnp.where(probs == p1, row_e, big), big), axis=0, keepdims=True)
    probs2 = jnp.where(in_grp, jnp.where(row_e == i1, -1.0, probs), -1.0)
    p2 = jnp.max(probs2, axis=0, keepdims=True)
    i2 = jnp.min(jnp.where(probs2 == p2, row_e, big), axis=0, keepdims=True)
    denom = p1 + p2
    gate1 = p_g * p1 / denom
    gate2 = p_g * p2 / denom

    used = jnp.where(row_e == i1, 1.0, jnp.where(row_e == i2, 1.0, 0.0))
    r = lax.broadcasted_iota(I32, (tm, tm), 0)
    c = lax.broadcasted_iota(I32, (tm, tm), 1)
    before = jnp.where(r < c, 1.0, 0.0).astype(BF16)
    excl = _dot(used.astype(BF16), before) + carry_sc[:, 0:1]
    rank1 = jnp.sum(jnp.where(row_e == i1, excl, 0.0), axis=0, keepdims=True)
    rank2 = jnp.sum(jnp.where(row_e == i2, excl, 0.0), axis=0, keepdims=True)
    carry_sc[...] = carry_sc[...] + jnp.sum(used, axis=1, keepdims=True)
    cnt_ref[...] = carry_sc[...]

    mi_ref[...] = jnp.zeros(mi_ref.shape, I32)
    mi_ref[0:1, :] = i1
    mi_ref[1:2, :] = i2
    mi_ref[2:3, :] = rank1.astype(I32)
    mi_ref[3:4, :] = rank2.astype(I32)
    mf_ref[...] = jnp.zeros(mf_ref.shape, F32)
    mf_ref[0:1, :] = gate1
    mf_ref[1:2, :] = gate2


def _route(lg_t, bias, tm):
    n = lg_t.shape[1]
    kern = functools.partial(_route_kernel, tm=tm)
    return pl.pallas_call(
        kern,
        grid=(n // tm,),
        in_specs=[pl.BlockSpec((LANES, tm), lambda i: (0, i)),
                  pl.BlockSpec((LANES, 1), lambda i: (0, 0))],
        out_specs=[pl.BlockSpec((8, tm), lambda i: (0, i)),
                   pl.BlockSpec((8, tm), lambda i: (0, i)),
                   pl.BlockSpec((N_EXPERTS, LANES), lambda i: (0, 0))],
        out_shape=[jax.ShapeDtypeStruct((8, n), I32), jax.ShapeDtypeStruct((8, n), F32),
                   jax.ShapeDtypeStruct((N_EXPERTS, LANES), F32)],
        scratch_shapes=[pltpu.VMEM((N_EXPERTS, LANES), F32)],
        compiler_params=pltpu.CompilerParams(dimension_semantics=("arbitrary",),
                                             vmem_limit_bytes=VMEM_LIMIT),
        name="route",
    )(lg_t, bias)


ROW_DMA_UNROLL = 8


def _dispatch_kernel(dest_ref, fill_ref, h2_ref, xs_hbm, zbuf, sem, zsem, *, tm, tb):
    def row_copy(t0, u, dst_row):
        src = h2_ref.at[pl.ds(t0, ROW_DMA_UNROLL)].at[pl.ds(u, 1)]
        return pltpu.make_async_copy(src, xs_hbm.at[pl.ds(dst_row, 1)], sem)

    def zero_copy(dst_row, rows):
        return pltpu.make_async_copy(zbuf.at[pl.ds(0, rows)], xs_hbm.at[pl.ds(dst_row, rows)], zsem)

    def zero_fill(start):
        def act(copy):
            copy.start() if start else copy.wait()

        def padding(e, carry):
            row, left, end = fill_ref[0, e], fill_ref[1, e], fill_ref[2, e]
            rows = tb // 2
            while rows >= 8:
                take = (left & rows) != 0
                end = end - jnp.where(take, rows, 0)
                pl.when(take)(functools.partial(lambda r, n: act(zero_copy(pl.multiple_of(r, 8), n)), end, rows))
                rows //= 2
            for i in range(7):
                pl.when(i < (left & 7))(functools.partial(lambda r: act(zero_copy(r, 1)), row + i))
            return carry

        def unused(i, carry):
            act(zero_copy(pl.multiple_of((fill_ref[0, N_EXPERTS] + i) * tb, tb), tb))
            return carry

        lax.fori_loop(0, N_EXPERTS, padding, 0)
        lax.fori_loop(0, fill_ref[1, N_EXPERTS], unused, 0)

    first = pl.program_id(0) == 0

    @pl.when(first)
    def _():
        zbuf[...] = jnp.zeros(zbuf.shape, zbuf.dtype)
        zero_fill(True)

    def issue(c, carry):
        t0 = pl.multiple_of(c * ROW_DMA_UNROLL, ROW_DMA_UNROLL)
        for u in range(ROW_DMA_UNROLL):
            for j in range(2):
                row_copy(t0, u, dest_ref[0, j * tm + t0 + u]).start()
        return carry

    def drain(t, carry):
        for j in range(2):
            row_copy(0, 0, 0).wait()
        return carry

    lax.fori_loop(0, tm // ROW_DMA_UNROLL, issue, 0)
    lax.fori_loop(0, tm, drain, 0, unroll=ROW_DMA_UNROLL)

    @pl.when(first)
    def _():
        zero_fill(False)


def _dispatch(dest, fill, h2, n_slots, tm, tb):
    n, d = h2.shape
    kern = functools.partial(_dispatch_kernel, tm=tm, tb=tb)
    return pl.pallas_call(
        kern,
        grid=(n // tm,),
        in_specs=[pl.BlockSpec((None, 1, 2 * tm), lambda i: (i, 0, 0), memory_space=pltpu.SMEM),
                  pl.BlockSpec(memory_space=pltpu.SMEM),
                  pl.BlockSpec((tm, d), lambda i: (i, 0))],
        out_specs=pl.BlockSpec(memory_space=pl.ANY),
        out_shape=jax.ShapeDtypeStruct((n_slots, d), F32),
        scratch_shapes=[pltpu.VMEM((tb, d), F32), pltpu.SemaphoreType.DMA(()), pltpu.SemaphoreType.DMA(())],
        compiler_params=pltpu.CompilerParams(dimension_semantics=("arbitrary",),
                                             has_side_effects=True, vmem_limit_bytes=VMEM_LIMIT),
        name="dispatch",
    )(dest, fill, h2)


def _expert_kernel(be_ref, nu_ref, x_ref, g_ref, w1_ref, w3_ref, w2_ref, y_ref):
    del be_ref

    @pl.when(pl.program_id(0) < nu_ref[0])
    def _():
        xb = _rms(x_ref[...], g_ref[...]).astype(BF16)
        a = _dot(xb, w1_ref[...])
        hb = (a * jax.nn.sigmoid(a)) * _dot(xb, w3_ref[...])
        y_ref[...] = _dot(hb.astype(BF16), w2_ref[...])

    @pl.when(pl.program_id(0) >= nu_ref[0])
    def _():
        y_ref[...] = jnp.zeros(y_ref.shape, y_ref.dtype)


def _experts(block_expert, n_used, xs, gain, w1, w3, w2, tb):
    n_slots, d = xs.shape
    de = w1.shape[-1]
    nb = n_slots // tb
    row = lambda i, be, nu: (i, 0)
    grid_spec = pltpu.PrefetchScalarGridSpec(
        num_scalar_prefetch=2,
        grid=(nb,),
        in_specs=[pl.BlockSpec((tb, d), row),
                  pl.BlockSpec((1, d), lambda i, be, nu: (0, 0)),
                  pl.BlockSpec((None, d, de), lambda i, be, nu: (be[i], 0, 0)),
                  pl.BlockSpec((None, d, de), lambda i, be, nu: (be[i], 0, 0)),
                  pl.BlockSpec((None, de, d), lambda i, be, nu: (be[i], 0, 0))],
        out_specs=pl.BlockSpec((tb, d), row),
    )
    return pl.pallas_call(
        _expert_kernel,
        grid_spec=grid_spec,
        out_shape=jax.ShapeDtypeStruct((n_slots, d), F32),
        compiler_params=pltpu.CompilerParams(dimension_semantics=("arbitrary",),
                                             vmem_limit_bytes=VMEM_LIMIT),
        name="experts",
    )(block_expert, n_used, xs, gain, w1, w3, w2)


def _combine_kernel(dest_ref, mf_ref, h2_ref, g_ref, y_hbm, o_ref, ybuf, sem, *, tm):
    def row_copy(src_row, j, t0, u):
        dst = ybuf.at[j, pl.ds(t0, ROW_DMA_UNROLL)].at[pl.ds(u, 1)]
        return pltpu.make_async_copy(y_hbm.at[pl.ds(src_row, 1)], dst, sem)

    def issue(c, carry):
        t0 = pl.multiple_of(c * ROW_DMA_UNROLL, ROW_DMA_UNROLL)
        for u in range(ROW_DMA_UNROLL):
            for j in range(2):
                row_copy(dest_ref[0, j * tm + t0 + u], j, t0, u).start()
        return carry

    def drain(t, carry):
        for j in range(2):
            row_copy(0, j, 0, 0).wait()
        return carry

    lax.fori_loop(0, tm // ROW_DMA_UNROLL, issue, 0)
    lax.fori_loop(0, tm, drain, 0, unroll=ROW_DMA_UNROLL)

    gates = jnp.concatenate([mf_ref[...], jnp.zeros((LANES - 8, tm), F32)], axis=0)
    gt = jnp.transpose(gates)
    h = h2_ref[...] + gt[:, 0:1] * ybuf[0] + gt[:, 1:2] * ybuf[1]
    o_ref[...] = _rms(h, g_ref[...])


def _combine(dest, mf, h2, gain, y, tm):
    n, d = h2.shape
    kern = functools.partial(_combine_kernel, tm=tm)
    return pl.pallas_call(
        kern,
        grid=(n // tm,),
        in_specs=[pl.BlockSpec((None, 1, 2 * tm), lambda i: (i, 0, 0), memory_space=pltpu.SMEM),
                  pl.BlockSpec((8, tm), lambda i: (0, i)),
                  pl.BlockSpec((tm, d), lambda i: (i, 0)),
                  pl.BlockSpec((1, d), lambda i: (0, 0)),
                  pl.BlockSpec(memory_space=pl.ANY)],
        out_specs=pl.BlockSpec((tm, d), lambda i: (i, 0)),
        out_shape=jax.ShapeDtypeStruct((n, d), F32),
        scratch_shapes=[pltpu.VMEM((2, tm, d), F32), pltpu.SemaphoreType.DMA(())],
        compiler_params=pltpu.CompilerParams(dimension_semantics=("arbitrary",),
                                             vmem_limit_bytes=VMEM_LIMIT),
        name="combine",
    )(dest, mf, h2, gain, y)


def _sigma_parts(n_heads):
    sigma = jnp.asarray([2.0 ** (-8.0 * (i + 1) / n_heads) for i in range(n_heads)], dtype=F32) * LOG2E
    s1 = sigma.astype(BF16).astype(F32)
    s2 = (sigma - s1).astype(BF16).astype(F32)
    s3 = ((sigma - s1) - s2).astype(BF16).astype(F32)
    return jnp.stack([s1, s2, s3], axis=1).reshape(-1)


def _prep_w_in(w_in):
    splits = []
    acc = 0
    for w in IN_WIDTHS[:-1]:
        acc += w
        splits.append(acc)
    qa, ka, va, qb, kb, vb, qi, ki, wi = jnp.split(w_in, splits, axis=1)
    w_n = jnp.concatenate([ka, kb, kb, ki, ki], axis=1)
    pad = jnp.zeros((w_in.shape[0], WIT_ROWS - IN_WIDTHS[-1]), w_in.dtype)
    w_t = jnp.concatenate([qa * (DH_A ** -0.5 * LOG2E), qb * (DH_B ** -0.5 * LOG2E), qi, va, vb,
                           wi * (H_I ** -0.5 * D_I ** -0.5), pad], axis=1).T
    return w_n.astype(BF16), w_t.astype(BF16)


def _prep_router(w_group, w_router):
    d = w_group.shape[0]
    w = jnp.concatenate([w_router, w_group, jnp.zeros((d, LANES - N_EXPERTS - N_GROUPS), F32)], axis=1).T
    hi = w.astype(BF16)
    lo = (w - hi.astype(F32)).astype(BF16)
    return hi, lo


def _block_sizes(s):
    return dict(tm_proj=512, tq=256, tk=TK, tm_merge=min(512, s), tm_route=512, tm_disp=512, tb=512, tm_comb=256)


def kernel(x, mem, mix_norm, w_in, lam_q1, lam_k1, lam_q2, lam_k2, diff_subln, w_branch_a, w_branch_b, w_gate,
           b_gate, w_out, cross_norm, mem_norm, w_cq, w_ckv, w_co, ffn_norm, w_group, b_group, w_router,
           b_router, w1, w3, w2, final_norm):
    b, s, d = x.shape
    n = b * s
    m = mem.shape[1]
    bs = _block_sizes(s)
    k_sel = min(TOPK_MAX, s // 4)
    row = lambda v: v.reshape(1, -1).astype(F32)

    w_n, w_t = _prep_w_in(w_in[0])
    ka, kb2, kik, qat, qbt, qit, vat, vbt, wit = _proj(x.reshape(n, d), row(mix_norm[0]), w_n, w_t, bs["tm_proj"])
    ka, kb2, kik = [o.reshape(b, s, -1) for o in (ka, kb2, kik)]
    qat, qbt, qit, vat, vbt = [o.reshape(b, s // TK, -1, TK) for o in (qat, qbt, qit, vat, vbt)]
    lam = (jnp.exp(jnp.sum(lam_q1[0].astype(F32) * lam_k1[0].astype(F32)))
           - jnp.exp(jnp.sum(lam_q2[0].astype(F32) * lam_k2[0].astype(F32))) + LAMBDA_INIT).reshape(1)
    subln_t = jnp.broadcast_to(diff_subln[0].astype(F32)[:, None], (LANES, LANES))
    ya = _diff_attention(qat, ka, vat, _sigma_parts(H_A), lam, subln_t, bs["tq"], bs["tk"])
    yb = _dsa_attention(qbt, kb2, vbt, qit, kik, wit, _sigma_parts(H_B), k_sel, bs["tq"], bs["tk"])

    mkv = _memkv(mem.reshape(b * m, d), row(mem_norm[0]), w_ckv[0].astype(BF16), min(512, b * m))
    w_r_hi, w_r_lo = _prep_router(w_group[0], w_router[0])
    params = dict(mix_g=row(mix_norm[0]), w_gate=w_gate[0].astype(BF16), b_gate=row(b_gate[0]),
                  w_a=w_branch_a[0].astype(BF16), w_b=w_branch_b[0].astype(BF16), w_out=w_out[0].astype(BF16),
                  cross_g=row(cross_norm[0]), w_cq=(w_cq[0] * (d // H_X) ** -0.5).astype(BF16),
                  w_co=w_co[0].astype(BF16), ffn_g=row(ffn_norm[0]), w_r_hi=w_r_hi, w_r_lo=w_r_lo)
    h2, lg_t = _merge(x, ya, yb, mkv.reshape(b, m, 2 * d), params, bs["tm_merge"])
    h2 = h2.reshape(n, d)

    bias = jnp.concatenate([b_router[0], b_group[0], jnp.zeros((LANES - N_EXPERTS - N_GROUPS,), F32)])
    mi, mf, cnt = _route(lg_t, bias.reshape(LANES, 1).astype(F32), bs["tm_route"])
    tb = bs["tb"]
    counts = cnt[:, 0].astype(I32)
    padded = (counts + tb - 1) // tb * tb
    pad_end = jnp.cumsum(padded)
    pad_start = (pad_end - padded).astype(I32)
    n_blocks = (2 * n) // tb + N_EXPERTS
    blk_first = jnp.arange(n_blocks, dtype=I32) * tb
    block_expert = jnp.minimum(jnp.sum((pad_end[None, :] <= blk_first[:, None]).astype(I32), axis=1), N_EXPERTS - 1)
    n_used = (pad_end[-1] // tb).astype(I32).reshape(1)
    start_of = jnp.sum(jnp.where(mi[0:2][None] == jnp.arange(N_EXPERTS, dtype=I32)[:, None, None],
                                 pad_start[:, None, None], 0), axis=0)
    dest = start_of + mi[2:4]

    def per_tile(tm):
        return dest.reshape(2, n // tm, tm).transpose(1, 0, 2).reshape(n // tm, 1, 2 * tm)

    fill = jnp.stack([jnp.concatenate([pad_start + counts, n_used]),
                      jnp.concatenate([padded - counts, n_blocks - n_used]),
                      jnp.concatenate([pad_end, n_used])]).astype(I32)
    xs = _dispatch(per_tile(bs["tm_disp"]), fill, h2, n_blocks * tb, bs["tm_disp"], tb)
    y = _experts(block_expert, n_used, xs, row(ffn_norm[0]), w1[0].astype(BF16), w3[0].astype(BF16),
                 w2[0].astype(BF16), tb)
    out = _combine(per_tile(bs["tm_comb"]), mf, h2, row(final_norm), y, bs["tm_comb"])
    return out.reshape(b, s, d)
```

```python
import functools

import jax
import jax.numpy as jnp
from jax import lax
from jax.experimental import pallas as pl
from jax.experimental.pallas import tpu as pltpu

F32 = jnp.float32
BF16 = jnp.bfloat16
I32 = jnp.int32

EPS = 1e-6
CHUNK_SHIFT = 6
H_A, DH_A = 4, 64
H_B, DH_B = 8, 64
H_I, D_I = 4, 64
TOPK_MAX = 256
H_X = 4
N_GROUPS, EXP_PER_GROUP = 4, 8
N_EXPERTS = N_GROUPS * EXP_PER_GROUP
LAMBDA_INIT = 0.8 - 0.6 * 1.0
LANES = 128
NEG = -1e30
INT_MIN = -2147483648
KEY_NEG_INF = -2139095041
VMEM_LIMIT = 56 * 1024 * 1024

LOG2E = 1.4426950408889634
TK = 256
WIT_ROWS = 16
SUM_ROWS = 16
IN_WIDTHS = (512, 512, 512, 512, 64, 64, 256, 64, 4)


def _rms(x, g):
    ms = jnp.mean(x * x, axis=-1, keepdims=True)
    return (x * lax.rsqrt(ms + EPS)) * g


def _dot_nt(a, b):
    return lax.dot_general(a, b, (((1,), (1,)), ((), ())), preferred_element_type=F32)


def _dot(a, b):
    return jnp.dot(a, b, preferred_element_type=F32)


def _proj_kernel(x_ref, g_ref, wn_ref, wt_ref, ka_ref, kb_ref, kik_ref,
                 qat_ref, qbt_ref, qit_ref, vat_ref, vbt_ref, wit_ref):
    xn = _rms(x_ref[...], g_ref[...]).astype(BF16)
    col = 0
    for ref in (ka_ref, kb_ref, kik_ref):
        n = ref.shape[-1]
        ref[...] = _dot(xn, wn_ref[:, col:col + n]).astype(ref.dtype)
        col += n
    row = 0
    for ref in (qat_ref, qbt_ref, qit_ref, vat_ref, vbt_ref):
        r = ref.shape[1]
        yt = _dot_nt(wt_ref[row:row + r, :], xn)
        for c in range(ref.shape[0]):
            ref[c] = yt[:, c * TK:(c + 1) * TK].astype(ref.dtype)
        row += r
    wit_ref[...] = _dot_nt(wt_ref[row:row + wit_ref.shape[0], :], xn)


def _proj(x2, gain, w_n, w_t, tm):
    n, d = x2.shape
    kb = tm // TK
    tok = lambda w: pl.BlockSpec((tm, w), lambda i: (i, 0))
    widths = (H_A * 2 * DH_A, LANES, LANES)
    rows_t = (H_A * 2 * DH_A, H_B * DH_B, H_I * D_I, H_A * 2 * DH_A, DH_B)
    return pl.pallas_call(
        _proj_kernel,
        grid=(n // tm,),
        in_specs=[pl.BlockSpec((tm, d), lambda i: (i, 0)),
                  pl.BlockSpec((1, d), lambda i: (0, 0)),
                  pl.BlockSpec(w_n.shape, lambda i: (0, 0)),
                  pl.BlockSpec(w_t.shape, lambda i: (0, 0))],
        out_specs=[tok(w) for w in widths] + [pl.BlockSpec((kb, r, TK), lambda i: (i, 0, 0)) for r in rows_t] + [
            pl.BlockSpec((WIT_ROWS, tm), lambda i: (0, i))],
        out_shape=[jax.ShapeDtypeStruct((n, w), BF16) for w in widths] + [
            jax.ShapeDtypeStruct((n // TK, r, TK), BF16) for r in rows_t] + [
            jax.ShapeDtypeStruct((WIT_ROWS, n), F32)],
        compiler_params=pltpu.CompilerParams(dimension_semantics=("arbitrary",),
                                             vmem_limit_bytes=VMEM_LIMIT),
        name="proj",
    )(x2, gain, w_n, w_t)


def _aug_query(sig_ref, h, tq):
    lane = lax.broadcasted_iota(I32, (tq, LANES), 1)
    i = lax.broadcasted_iota(I32, (tq, LANES), 0).astype(F32)
    s1, s2, s3 = sig_ref[3 * h], sig_ref[3 * h + 1], sig_ref[3 * h + 2]
    c = ((s1 + s2) + s3) * i
    c1 = c.astype(BF16).astype(F32)
    c2 = (c - c1).astype(BF16).astype(F32)
    c3 = (c - c1) - c2
    out = jnp.zeros((tq, LANES), F32)
    for n, v in enumerate((c1, c2, c3, s1, s2, s3, s1, s2, s3, -s1, -s2, -s3)):
        out = jnp.where(lane == n, v, out)
    return out.astype(BF16)


def _aug_queries(sig, n_heads, tq):
    return jnp.stack([_aug_query(sig, h, tq).T for h in range(n_heads)])


def _aug_key(tk, off):
    lane = lax.broadcasted_iota(I32, (tk, LANES), 1)
    j = lax.broadcasted_iota(I32, (tk, LANES), 0)
    j_lo = (j & 255).astype(F32)
    j_hi = (j - (j & 255)).astype(F32)
    base = jnp.where(lane < 3, -1.0, jnp.where(lane < 6, j_lo, jnp.where(lane < 9, j_hi, jnp.where(lane < 12, off, 0.0))))
    return base.astype(BF16)


def _sigma(sig_ref, h):
    return (sig_ref[3 * h] + sig_ref[3 * h + 1]) + sig_ref[3 * h + 2]


def _with_ones_rows(vt):
    return jnp.concatenate([vt, jnp.ones((SUM_ROWS, vt.shape[1]), vt.dtype)], axis=0)


def _softmax_step(r, st, vt_ones, m_sc, acc_sc):
    m_prev = m_sc[r][0:1, :]
    m_next = jnp.maximum(m_prev, jnp.max(st, axis=0, keepdims=True))
    p = jnp.exp2(st - m_next)
    alpha = jnp.exp2(m_prev - m_next)
    acc_sc[r] = alpha * acc_sc[r] + _dot(vt_ones, p.astype(BF16))
    m_sc[r] = jnp.broadcast_to(m_next, m_sc.shape[1:])


def _diff_kernel(sig_ref, lam_ref, q_ref, k_ref, vt_ref, g_ref, augq_ref, o_ref,
                 qaug_sc, s_sc, m_sc, acc_sc, *, tq, tk):
    qblk = pl.program_id(1)
    lam = lam_ref[0]
    lane_k = lax.broadcasted_iota(I32, (tk, LANES), 1)
    for h in range(H_A):
        qaug_sc[h] = jnp.concatenate([q_ref[h * LANES:(h + 1) * LANES, :], augq_ref[h]], axis=0)
    m_sc[...] = jnp.full(m_sc.shape, NEG, F32)
    acc_sc[...] = jnp.zeros(acc_sc.shape, F32)

    def block(kb, diag):
        off = pl.multiple_of(kb * tk, tk)
        k = k_ref[pl.ds(off, tk), :]
        rel = qblk * tq - kb * tk
        augk = _aug_key(tk, rel.astype(F32))
        if diag:
            jj = lax.broadcasted_iota(I32, (tk, tq), 0)
            ii = lax.broadcasted_iota(I32, (tk, tq), 1)
            allowed = ((off + jj) >> CHUNK_SHIFT) <= ((qblk * tq + ii) >> CHUNK_SHIFT)
            ahead = jnp.maximum(jj - ii - rel, 0).astype(F32)
        for h in range(H_A):
            kh = k[:, h * LANES:(h + 1) * LANES]
            zero = jnp.zeros_like(kh)
            qa = qaug_sc[h]
            for m in range(2):
                km = jnp.where(lane_k < DH_A if m == 0 else lane_k >= DH_A, kh, zero)
                st = _dot(jnp.concatenate([km, augk], axis=1), qa)
                if diag:
                    st = jnp.where(allowed, st - (2.0 * _sigma(sig_ref, h)) * ahead, NEG)
                s_sc[2 * h + m] = st
        for h in range(H_A):
            vt = _with_ones_rows(vt_ref[kb, h * LANES:(h + 1) * LANES, :])
            for m in range(2):
                _softmax_step(2 * h + m, s_sc[2 * h + m], vt, m_sc, acc_sc)

    def off_diagonal(kb, carry):
        block(kb, False)
        return carry

    n_before = (qblk * tq) // tk
    lax.fori_loop(0, n_before, off_diagonal, 0)
    for c in range(max(1, tq // tk)):
        block(n_before + c, True)

    gain = jnp.concatenate([g_ref[...]] * (tq // LANES), axis=1)
    for h in range(H_A):
        a0, a1 = acc_sc[2 * h], acc_sc[2 * h + 1]
        y = a0[:LANES] / a0[LANES:LANES + 1] - lam * (a1[:LANES] / a1[LANES:LANES + 1])
        ms = jnp.mean(y * y, axis=0, keepdims=True)
        yn = (y * lax.rsqrt(ms + EPS)) * gain
        o_ref[:, h * LANES:(h + 1) * LANES] = (jnp.transpose(yn) * (1.0 - LAMBDA_INIT)).astype(o_ref.dtype)


def _diff_attention(qat, ka, vat, sig, lam, subln_t, tq, tk):
    b, s, w = ka.shape
    assert tq % tk == 0 or tk % tq == 0
    kern = functools.partial(_diff_kernel, tq=tq, tk=tk)
    smem = pl.BlockSpec(memory_space=pltpu.SMEM)
    return pl.pallas_call(
        kern,
        grid=(b, s // tq),
        in_specs=[smem, smem,
                  pl.BlockSpec((None, None, w, tq), lambda bi, i: (bi, i, 0, 0)),
                  pl.BlockSpec((None, s, w), lambda bi, i: (bi, 0, 0)),
                  pl.BlockSpec((None, s // tk, w, tk), lambda bi, i: (bi, 0, 0, 0)),
                  pl.BlockSpec((LANES, LANES), lambda bi, i: (0, 0)),
                  pl.BlockSpec((H_A, LANES, tq), lambda bi, i: (0, 0, 0))],
        out_specs=pl.BlockSpec((None, tq, w), lambda bi, i: (bi, i, 0)),
        out_shape=jax.ShapeDtypeStruct((b, s, w), BF16),
        scratch_shapes=[pltpu.VMEM((H_A, 2 * LANES, tq), BF16),
                        pltpu.VMEM((2 * H_A, tk, tq), F32),
                        pltpu.VMEM((2 * H_A, 8, tq), F32),
                        pltpu.VMEM((2 * H_A, LANES + SUM_ROWS, tq), F32)],
        compiler_params=pltpu.CompilerParams(dimension_semantics=("arbitrary",) * 2,
                                             vmem_limit_bytes=VMEM_LIMIT),
        name="diff_attn",
    )(sig, lam, qat, ka, vat, subln_t, _aug_queries(sig, H_A, tq))


def _dsa_kernel(sig_ref, qb_ref, kb_ref, vt_ref, qi_ref, kik_ref, wit_ref, augq_ref, tri_ref, o_ref,
                qaug_sc, key_sc, v16_sc, thr_sc, need_sc, s_sc, m_sc, acc_sc, *, tq, tk, k_sel):
    qblk = pl.program_id(1)
    n_before = (qblk * tq) // tk
    nvis = n_before + max(1, tq // tk)
    lane_k = lax.broadcasted_iota(I32, (tk, LANES), 1)
    jj = lax.broadcasted_iota(I32, (tk, tq), 0)
    ii = lax.broadcasted_iota(I32, (tk, tq), 1)
    t_chunk = (qblk * tq + ii) >> CHUNK_SHIFT
    w_idx = wit_ref[...]
    qidx = qi_ref[...]

    def index_block(kb, carry):
        off = pl.multiple_of(kb * tk, tk)
        kik = kik_ref[pl.ds(off, tk), :]
        zero = jnp.zeros_like(kik)
        k_half = (jnp.where(lane_k < D_I, kik, zero), jnp.where(lane_k >= D_I, kik, zero))
        isc = jnp.zeros((tk, tq), F32)
        for h in range(H_I):
            d = _dot(k_half[h % 2], qidx[(h // 2) * LANES:(h // 2 + 1) * LANES, :])
            isc = isc + w_idx[h:h + 1, :] * jnp.maximum(d, 0.0)
        bits = lax.bitcast_convert_type(isc, I32)
        key = jnp.where(bits < 0, bits ^ 0x7FFFFFFF, bits)
        key = jnp.where(isc == 0.0, 0, key)
        allowed = ((off + jj) >> CHUNK_SHIFT) <= t_chunk
        key = jnp.where(allowed, key, KEY_NEG_INF)
        key_sc[kb] = key
        hi = key >> 16
        top = lax.shift_left(jnp.where(hi >= 0, hi, hi ^ 0x7FFF), 16)
        v16_sc[kb] = lax.bitcast_convert_type(top, F32).astype(BF16)
        return carry

    lax.fori_loop(0, nvis, index_block, 0)

    kf = float(k_sel)
    zero_i = jnp.zeros((1, tq), I32)

    def count_ge(cand):
        def body(kb, acc):
            hit = jnp.where(key_sc[kb] >= cand, 1.0, 0.0)
            return acc + jnp.sum(hit.reshape(tk // 8, 8, tq), axis=0)
        acc = lax.fori_loop(0, nvis, body, jnp.zeros((8, tq), F32))
        return jnp.sum(acc, axis=0, keepdims=True)

    def count16_ge(hi):
        hi = jnp.maximum(hi, KEY_NEG_INF >> 16)
        hi = jnp.where((hi > 0) & (hi < 128), 128, jnp.where((hi < 0) & (hi >= -128), 0, hi))
        top = lax.shift_left(jnp.where(hi >= 0, hi, hi ^ 0x7FFF), 16)
        cand = lax.bitcast_convert_type(top, F32).astype(BF16)
        one = jnp.ones((tk, tq), BF16)
        zero = jnp.zeros((tk, tq), BF16)

        def body(kb, acc):
            hit = jnp.where(v16_sc[kb] >= cand, one, zero)
            for r in range(tk // 16):
                acc = acc + hit[r * 16:(r + 1) * 16]
            return acc
        acc = lax.fori_loop(0, nvis, body, jnp.zeros((16, tq), BF16))
        return jnp.sum(acc.astype(F32), axis=0, keepdims=True)

    t16 = jnp.where(count16_ge(zero_i) >= kf, zero_i, jnp.full((1, tq), -32768, I32))

    def bit16_step(i, t):
        cand = t + lax.shift_left(jnp.int32(1), jnp.asarray(14 - i, I32))
        return jnp.where(count16_ge(cand) >= kf, cand, t)

    t16 = lax.fori_loop(0, 15, bit16_step, t16)

    def bit_step(i, t):
        cand = t + lax.shift_left(jnp.int32(1), jnp.asarray(15 - i, I32))
        return jnp.where(count_ge(cand) >= kf, cand, t)

    thr = lax.fori_loop(0, 16, bit_step, lax.shift_left(t16, 16))

    thr_sc[...] = jnp.broadcast_to(thr, thr_sc.shape)
    above = count_ge(thr + 1)
    need_sc[...] = jnp.broadcast_to(kf - above, need_sc.shape)
    off_rows = jnp.where(count_ge(thr) >= kf, jnp.where(above < kf, 0.0, 1.0), 1.0)

    @pl.when(jnp.max(off_rows) > 0.0)
    def _():
        def full_step(i, t):
            cand = t + lax.shift_left(jnp.int32(1), jnp.asarray(30 - i, I32))
            return jnp.where(count_ge(cand) >= kf, cand, t)
        t0 = jnp.where(count_ge(zero_i) >= kf, zero_i, jnp.full((1, tq), INT_MIN, I32))
        t_full = lax.fori_loop(0, 31, full_step, t0)
        thr_sc[...] = jnp.broadcast_to(t_full, thr_sc.shape)
        need_sc[...] = jnp.broadcast_to(kf - count_ge(t_full + 1), need_sc.shape)

    thr = thr_sc[0:1, :]

    need = need_sc[0:1, :]
    thr_eq = jnp.where(thr == KEY_NEG_INF, INT_MIN, thr)

    for h in range(H_B):
        qaug_sc[h] = jnp.concatenate([qb_ref[(h // 2) * LANES:(h // 2 + 1) * LANES, :], augq_ref[h]], axis=0)
    m_sc[...] = jnp.full(m_sc.shape, NEG, F32)
    acc_sc[...] = jnp.zeros(acc_sc.shape, F32)

    def attend(kb, diag, ties_before):
        off = pl.multiple_of(kb * tk, tk)
        kk = kb_ref[pl.ds(off, tk), :]
        zero = jnp.zeros_like(kk)
        rel = qblk * tq - kb * tk
        augk = _aug_key(tk, rel.astype(F32))
        k_aug = (jnp.concatenate([jnp.where(lane_k < DH_B, kk, zero), augk], axis=1),
                 jnp.concatenate([jnp.where(lane_k >= DH_B, kk, zero), augk], axis=1))
        vt = _with_ones_rows(vt_ref[kb])
        key = key_sc[kb]
        tie = key == thr_eq
        tie_rank = _dot(tri_ref[...], jnp.where(tie, 1.0, 0.0).astype(BF16)) + ties_before
        mask = jnp.where(key > thr, 0.0, jnp.where(tie, jnp.where(tie_rank <= need, 0.0, NEG), NEG))
        if diag:
            ahead = jnp.maximum(jj - ii - rel, 0).astype(F32)
        for h in range(H_B):
            st = _dot(k_aug[h % 2], qaug_sc[h]) + mask
            if diag:
                st = st - (2.0 * _sigma(sig_ref, h)) * ahead
            s_sc[h] = st
        for h in range(H_B):
            _softmax_step(h, s_sc[h], vt, m_sc, acc_sc)
        return tie_rank[tk - 1:tk, :]

    ties = lax.fori_loop(0, n_before, lambda kb, t: attend(kb, False, t), jnp.zeros((1, tq), F32))
    for c in range(max(1, tq // tk)):
        ties = attend(n_before + c, True, ties)

    yt = jnp.concatenate([acc_sc[h][:DH_B] / acc_sc[h][DH_B:DH_B + 1] for h in range(H_B)], axis=0)
    o_ref[...] = jnp.transpose(yt).astype(o_ref.dtype)


def _dsa_attention(qbt, kb2, vbt, qit, kik, wit, sig, k_sel, tq, tk):
    b, s, _ = kb2.shape
    w = H_B * DH_B
    assert tq % tk == 0 or tk % tq == 0
    kern = functools.partial(_dsa_kernel, tq=tq, tk=tk, k_sel=k_sel)
    tri = (jnp.arange(tk)[None, :] <= jnp.arange(tk)[:, None]).astype(BF16)
    tile = lambda c: pl.BlockSpec((None, tq, c), lambda bi, i: (bi, i, 0))
    tile_t = lambda r: pl.BlockSpec((None, None, r, tq), lambda bi, i: (bi, i, 0, 0))
    full = lambda c: pl.BlockSpec((None, s, c), lambda bi, i: (bi, 0, 0))
    nq = s // tq
    return pl.pallas_call(
        kern,
        grid=(b, nq),
        in_specs=[pl.BlockSpec(memory_space=pltpu.SMEM),
                  tile_t(w), full(LANES),
                  pl.BlockSpec((None, s // tk, DH_B, tk), lambda bi, i: (bi, 0, 0, 0)),
                  tile_t(H_I * D_I), full(LANES),
                  pl.BlockSpec((WIT_ROWS, tq), lambda bi, i: (0, bi * nq + i)),
                  pl.BlockSpec((H_B, LANES, tq), lambda bi, i: (0, 0, 0)),
                  pl.BlockSpec((tk, tk), lambda bi, i: (0, 0))],
        out_specs=tile(w),
        out_shape=jax.ShapeDtypeStruct((b, s, w), BF16),
        scratch_shapes=[pltpu.VMEM((H_B, 2 * LANES, tq), BF16),
                        pltpu.VMEM((s // tk, tk, tq), I32),
                        pltpu.VMEM((s // tk, tk, tq), BF16),
                        pltpu.VMEM((8, tq), I32),
                        pltpu.VMEM((8, tq), F32),
                        pltpu.VMEM((H_B, tk, tq), F32),
                        pltpu.VMEM((H_B, 8, tq), F32),
                        pltpu.VMEM((H_B, DH_B + SUM_ROWS, tq), F32)],
        compiler_params=pltpu.CompilerParams(dimension_semantics=("arbitrary",) * 2,
                                             vmem_limit_bytes=VMEM_LIMIT),
        name="dsa_attn",
    )(sig, qbt, kb2, vbt, qit, kik, wit, _aug_queries(sig, H_B, tq), tri)


def _memkv_kernel(m_ref, g_ref, w_ref, o_ref):
    mn = _rms(m_ref[...], g_ref[...]).astype(BF16)
    o_ref[...] = _dot(mn, w_ref[...]).astype(o_ref.dtype)


def _memkv(mem2, gain, w_ckv, tm):
    n, d = mem2.shape
    return pl.pallas_call(
        _memkv_kernel,
        grid=(n // tm,),
        in_specs=[pl.BlockSpec((tm, d), lambda i: (i, 0)),
                  pl.BlockSpec((1, d), lambda i: (0, 0)),
                  pl.BlockSpec(w_ckv.shape, lambda i: (0, 0))],
        out_specs=pl.BlockSpec((tm, w_ckv.shape[1]), lambda i: (i, 0)),
        out_shape=jax.ShapeDtypeStruct((n, w_ckv.shape[1]), BF16),
        compiler_params=pltpu.CompilerParams(dimension_semantics=("arbitrary",),
                                             vmem_limit_bytes=VMEM_LIMIT),
        name="memkv",
    )(mem2, gain, w_ckv)


def _merge_kernel(x_ref, ya_ref, yb_ref, mkv_ref, mixg_ref, wgate_ref, bgate_ref, wa_ref, wb_ref, wout_ref,
                  crossg_ref, wcq_ref, wco_ref, ffng_ref, wrh_ref, wrl_ref, h2_ref, lg_ref):
    d = x_ref.shape[-1]
    dh = d // H_X
    x = x_ref[...]
    xn = _rms(x, mixg_ref[...]).astype(BF16)
    gates = jax.nn.sigmoid(_dot(xn, wgate_ref[...]) + bgate_ref[...])
    merged = gates[:, :d] * _dot(ya_ref[...], wa_ref[...]) + gates[:, d:] * _dot(yb_ref[...], wb_ref[...])
    h1 = x + _dot(merged.astype(BF16), wout_ref[...])

    q = _dot(_rms(h1, crossg_ref[...]).astype(BF16), wcq_ref[...]).astype(BF16)
    heads = []
    for h in range(H_X):
        k = mkv_ref[:, h * dh:(h + 1) * dh]
        v = mkv_ref[:, d + h * dh:d + (h + 1) * dh]
        s = _dot_nt(q[:, h * dh:(h + 1) * dh], k)
        e = jnp.exp(s - jnp.max(s, axis=1, keepdims=True))
        p = e / jnp.sum(e, axis=1, keepdims=True)
        heads.append(_dot(p.astype(BF16), v).astype(BF16))
    h2 = h1 + _dot(jnp.concatenate(heads, axis=1), wco_ref[...])
    h2_ref[...] = h2

    f = _rms(h2, ffng_ref[...])
    f_hi = f.astype(BF16)
    f_lo = (f - f_hi.astype(F32)).astype(BF16)
    w_hi = wrh_ref[...]
    lg_ref[...] = _dot_nt(w_hi, f_hi) + _dot_nt(w_hi, f_lo) + _dot_nt(wrl_ref[...], f_hi)


def _merge(x3, ya, yb, mkv, p, tm):
    b, s, d = x3.shape
    m = mkv.shape[1]
    tok = lambda w: pl.BlockSpec((None, tm, w), lambda bi, i: (bi, i, 0))
    const = lambda a: pl.BlockSpec(a.shape, lambda bi, i: (0,) * a.ndim, pipeline_mode=pl.Buffered(1))
    consts = (p["mix_g"], p["w_gate"], p["b_gate"], p["w_a"], p["w_b"], p["w_out"], p["cross_g"],
              p["w_cq"], p["w_co"], p["ffn_g"], p["w_r_hi"], p["w_r_lo"])
    return pl.pallas_call(
        _merge_kernel,
        grid=(b, s // tm),
        in_specs=[tok(d), tok(ya.shape[-1]), tok(yb.shape[-1]),
                  pl.BlockSpec((None, m, 2 * d), lambda bi, i: (bi, 0, 0))] + [const(a) for a in consts],
        out_specs=[tok(d), pl.BlockSpec((LANES, tm), lambda bi, i: (0, bi * (s // tm) + i))],
        out_shape=[jax.ShapeDtypeStruct((b, s, d), F32), jax.ShapeDtypeStruct((LANES, b * s), F32)],
        compiler_params=pltpu.CompilerParams(dimension_semantics=("arbitrary",) * 2,
                                             vmem_limit_bytes=VMEM_LIMIT),
        name="merge",
    )(x3, ya, yb, mkv, *consts)


def _route_kernel(lg_ref, bias_ref, mi_ref, mf_ref, cnt_ref, carry_sc, *, tm):
    @pl.when(pl.program_id(0) == 0)
    def _():
        carry_sc[...] = jnp.zeros(carry_sc.shape, F32)

    lg = lg_ref[...] + bias_ref[...]
    e = lg[0:N_EXPERTS]
    g = lg[N_EXPERTS:N_EXPERTS + 8]
    row_g = lax.broadcasted_iota(I32, (8, tm), 0)
    g = jnp.where(row_g < N_GROUPS, g, -jnp.inf)
    gmax = jnp.max(g, axis=0, keepdims=True)
    g_idx = jnp.min(jnp.where(g == gmax, row_g, N_GROUPS), axis=0, keepdims=True)
    p_g = 1.0 / jnp.sum(jnp.exp(g - gmax), axis=0, keepdims=True)

    row_e = lax.broadcasted_iota(I32, (N_EXPERTS, tm), 0)
    in_grp = (row_e >> 3) == g_idx
    emax = jnp.max(jnp.where(in_grp, e, -jnp.inf), axis=0, keepdims=True)
    ex = jnp.where(in_grp, jnp.exp(jnp.where(in_grp, e - emax, 0.0)), 0.0)
    probs = ex / jnp.sum(ex, axis=0, keepdims=True)
    big = N_EXPERTS
    p1 = jnp.max(probs, axis=0, keepdims=True)
    i1 = jnp.min(jnp.where(in_grp, jnp.where(probs == p1, row_e, big), big), axis=0, keepdims=True)
    probs2 = jnp.where(in_grp, jnp.where(row_e == i1, -1.0, probs), -1.0)
    p2 = jnp.max(probs2, axis=0, keepdims=True)
    i2 = jnp.min(jnp.where(probs2 == p2, row_e, big), axis=0, keepdims=True)
    denom = p1 + p2
    gate1 = p_g * p1 / denom
    gate2 = p_g * p2 / denom

    used = jnp.where(row_e == i1, 1.0, jnp.where(row_e == i2, 1.0, 0.0))
    r = lax.broadcasted_iota(I32, (tm, tm), 0)
    c = lax.broadcasted_iota(I32, (tm, tm), 1)
    before = jnp.where(r < c, 1.0, 0.0).astype(BF16)
    excl = _dot(used.astype(BF16), before) + carry_sc[:, 0:1]
    rank1 = jnp.sum(jnp.where(row_e == i1, excl, 0.0), axis=0, keepdims=True)
    rank2 = jnp.sum(jnp.where(row_e == i2, excl, 0.0), axis=0, keepdims=True)
    carry_sc[...] = carry_sc[...] + jnp.sum(used, axis=1, keepdims=True)
    cnt_ref[...] = carry_sc[...]

    mi_ref[...] = jnp.zeros(mi_ref.shape, I32)
    mi_ref[0:1, :] = i1
    mi_ref[1:2, :] = i2
    mi_ref[2:3, :] = rank1.astype(I32)
    mi_ref[3:4, :] = rank2.astype(I32)
    mf_ref[...] = jnp.zeros(mf_ref.shape, F32)
    mf_ref[0:1, :] = gate1
    mf_ref[1:2, :] = gate2


def _route(lg_t, bias, tm):
    n = lg_t.shape[1]
    kern = functools.partial(_route_kernel, tm=tm)
    return pl.pallas_call(
        kern,
        grid=(n // tm,),
        in_specs=[pl.BlockSpec((LANES, tm), lambda i: (0, i)),
                  pl.BlockSpec((LANES, 1), lambda i: (0, 0))],
        out_specs=[pl.BlockSpec((8, tm), lambda i: (0, i)),
                   pl.BlockSpec((8, tm), lambda i: (0, i)),
                   pl.BlockSpec((N_EXPERTS, LANES), lambda i: (0, 0))],
        out_shape=[jax.ShapeDtypeStruct((8, n), I32), jax.ShapeDtypeStruct((8, n), F32),
                   jax.ShapeDtypeStruct((N_EXPERTS, LANES), F32)],
        scratch_shapes=[pltpu.VMEM((N_EXPERTS, LANES), F32)],
        compiler_params=pltpu.CompilerParams(dimension_semantics=("arbitrary",),
                                             vmem_limit_bytes=VMEM_LIMIT),
        name="route",
    )(lg_t, bias)


ROW_DMA_UNROLL = 8


def _dispatch_kernel(dest_ref, fill_ref, h2_ref, xs_hbm, zbuf, sem, zsem, *, tm, tb):
    def row_copy(t0, u, dst_row):
        src = h2_ref.at[pl.ds(t0, ROW_DMA_UNROLL)].at[pl.ds(u, 1)]
        return pltpu.make_async_copy(src, xs_hbm.at[pl.ds(dst_row, 1)], sem)

    def zero_copy(dst_row, rows):
        return pltpu.make_async_copy(zbuf.at[pl.ds(0, rows)], xs_hbm.at[pl.ds(dst_row, rows)], zsem)

    def zero_fill(start):
        def act(copy):
            copy.start() if start else copy.wait()

        def padding(e, carry):
            row, left, end = fill_ref[0, e], fill_ref[1, e], fill_ref[2, e]
            rows = tb // 2
            while rows >= 8:
                take = (left & rows) != 0
                end = end - jnp.where(take, rows, 0)
                pl.when(take)(functools.partial(lambda r, n: act(zero_copy(pl.multiple_of(r, 8), n)), end, rows))
                rows //= 2
            for i in range(7):
                pl.when(i < (left & 7))(functools.partial(lambda r: act(zero_copy(r, 1)), row + i))
            return carry

        def unused(i, carry):
            act(zero_copy(pl.multiple_of((fill_ref[0, N_EXPERTS] + i) * tb, tb), tb))
            return carry

        lax.fori_loop(0, N_EXPERTS, padding, 0)
        lax.fori_loop(0, fill_ref[1, N_EXPERTS], unused, 0)

    first = pl.program_id(0) == 0

    @pl.when(first)
    def _():
        zbuf[...] = jnp.zeros(zbuf.shape, zbuf.dtype)
        zero_fill(True)

    def issue(c, carry):
        t0 = pl.multiple_of(c * ROW_DMA_UNROLL, ROW_DMA_UNROLL)
        for u in range(ROW_DMA_UNROLL):
            for j in range(2):
                row_copy(t0, u, dest_ref[0, j * tm + t0 + u]).start()
        return carry

    def drain(t, carry):
        for j in range(2):
            row_copy(0, 0, 0).wait()
        return carry

    lax.fori_loop(0, tm // ROW_DMA_UNROLL, issue, 0)
    lax.fori_loop(0, tm, drain, 0, unroll=ROW_DMA_UNROLL)

    @pl.when(first)
    def _():
        zero_fill(False)


def _dispatch(dest, fill, h2, n_slots, tm, tb):
    n, d = h2.shape
    kern = functools.partial(_dispatch_kernel, tm=tm, tb=tb)
    return pl.pallas_call(
        kern,
        grid=(n // tm,),
        in_specs=[pl.BlockSpec((None, 1, 2 * tm), lambda i: (i, 0, 0), memory_space=pltpu.SMEM),
                  pl.BlockSpec(memory_space=pltpu.SMEM),
                  pl.BlockSpec((tm, d), lambda i: (i, 0))],
        out_specs=pl.BlockSpec(memory_space=pl.ANY),
        out_shape=jax.ShapeDtypeStruct((n_slots, d), F32),
        scratch_shapes=[pltpu.VMEM((tb, d), F32), pltpu.SemaphoreType.DMA(()), pltpu.SemaphoreType.DMA(())],
        compiler_params=pltpu.CompilerParams(dimension_semantics=("arbitrary",),
                                             has_side_effects=True, vmem_limit_bytes=VMEM_LIMIT),
        name="dispatch",
    )(dest, fill, h2)


def _expert_kernel(be_ref, nu_ref, x_ref, g_ref, w1_ref, w3_ref, w2_ref, y_ref):
    del be_ref

    @pl.when(pl.program_id(0) < nu_ref[0])
    def _():
        xb = _rms(x_ref[...], g_ref[...]).astype(BF16)
        a = _dot(xb, w1_ref[...])
        hb = (a * jax.nn.sigmoid(a)) * _dot(xb, w3_ref[...])
        y_ref[...] = _dot(hb.astype(BF16), w2_ref[...])

    @pl.when(pl.program_id(0) >= nu_ref[0])
    def _():
        y_ref[...] = jnp.zeros(y_ref.shape, y_ref.dtype)


def _experts(block_expert, n_used, xs, gain, w1, w3, w2, tb):
    n_slots, d = xs.shape
    de = w1.shape[-1]
    nb = n_slots // tb
    row = lambda i, be, nu: (i, 0)
    grid_spec = pltpu.PrefetchScalarGridSpec(
        num_scalar_prefetch=2,
        grid=(nb,),
        in_specs=[pl.BlockSpec((tb, d), row),
                  pl.BlockSpec((1, d), lambda i, be, nu: (0, 0)),
                  pl.BlockSpec((None, d, de), lambda i, be, nu: (be[i], 0, 0)),
                  pl.BlockSpec((None, d, de), lambda i, be, nu: (be[i], 0, 0)),
                  pl.BlockSpec((None, de, d), lambda i, be, nu: (be[i], 0, 0))],
        out_specs=pl.BlockSpec((tb, d), row),
    )
    return pl.pallas_call(
        _expert_kernel,
        grid_spec=grid_spec,
        out_shape=jax.ShapeDtypeStruct((n_slots, d), F32),
        compiler_params=pltpu.CompilerParams(dimension_semantics=("arbitrary",),
                                             vmem_limit_bytes=VMEM_LIMIT),
        name="experts",
    )(block_expert, n_used, xs, gain, w1, w3, w2)


def _combine_kernel(dest_ref, mf_ref, h2_ref, g_ref, y_hbm, o_ref, ybuf, sem, *, tm):
    def row_copy(src_row, j, t0, u):
        dst = ybuf.at[j, pl.ds(t0, ROW_DMA_UNROLL)].at[pl.ds(u, 1)]
        return pltpu.make_async_copy(y_hbm.at[pl.ds(src_row, 1)], dst, sem)

    def issue(c, carry):
        t0 = pl.multiple_of(c * ROW_DMA_UNROLL, ROW_DMA_UNROLL)
        for u in range(ROW_DMA_UNROLL):
            for j in range(2):
                row_copy(dest_ref[0, j * tm + t0 + u], j, t0, u).start()
        return carry

    def drain(t, carry):
        for j in range(2):
            row_copy(0, j, 0, 0).wait()
        return carry

    lax.fori_loop(0, tm // ROW_DMA_UNROLL, issue, 0)
    lax.fori_loop(0, tm, drain, 0, unroll=ROW_DMA_UNROLL)

    gates = jnp.concatenate([mf_ref[...], jnp.zeros((LANES - 8, tm), F32)], axis=0)
    gt = jnp.transpose(gates)
    h = h2_ref[...] + gt[:, 0:1] * ybuf[0] + gt[:, 1:2] * ybuf[1]
    o_ref[...] = _rms(h, g_ref[...])


def _combine(dest, mf, h2, gain, y, tm):
    n, d = h2.shape
    kern = functools.partial(_combine_kernel, tm=tm)
    return pl.pallas_call(
        kern,
        grid=(n // tm,),
        in_specs=[pl.BlockSpec((None, 1, 2 * tm), lambda i: (i, 0, 0), memory_space=pltpu.SMEM),
                  pl.BlockSpec((8, tm), lambda i: (0, i)),
                  pl.BlockSpec((tm, d), lambda i: (i, 0)),
                  pl.BlockSpec((1, d), lambda i: (0, 0)),
                  pl.BlockSpec(memory_space=pl.ANY)],
        out_specs=pl.BlockSpec((tm, d), lambda i: (i, 0)),
        out_shape=jax.ShapeDtypeStruct((n, d), F32),
        scratch_shapes=[pltpu.VMEM((2, tm, d), F32), pltpu.SemaphoreType.DMA(())],
        compiler_params=pltpu.CompilerParams(dimension_semantics=("arbitrary",),
                                             vmem_limit_bytes=VMEM_LIMIT),
        name="combine",
    )(dest, mf, h2, gain, y)


def _sigma_parts(n_heads):
    sigma = jnp.asarray([2.0 ** (-8.0 * (i + 1) / n_heads) for i in range(n_heads)], dtype=F32) * LOG2E
    s1 = sigma.astype(BF16).astype(F32)
    s2 = (sigma - s1).astype(BF16).astype(F32)
    s3 = ((sigma - s1) - s2).astype(BF16).astype(F32)
    return jnp.stack([s1, s2, s3], axis=1).reshape(-1)


def _prep_w_in(w_in):
    splits = []
    acc = 0
    for w in IN_WIDTHS[:-1]:
        acc += w
        splits.append(acc)
    qa, ka, va, qb, kb, vb, qi, ki, wi = jnp.split(w_in, splits, axis=1)
    w_n = jnp.concatenate([ka, kb, kb, ki, ki], axis=1)
    pad = jnp.zeros((w_in.shape[0], WIT_ROWS - IN_WIDTHS[-1]), w_in.dtype)
    w_t = jnp.concatenate([qa * (DH_A ** -0.5 * LOG2E), qb * (DH_B ** -0.5 * LOG2E), qi, va, vb,
                           wi * (H_I ** -0.5 * D_I ** -0.5), pad], axis=1).T
    return w_n.astype(BF16), w_t.astype(BF16)


def _prep_router(w_group, w_router):
    d = w_group.shape[0]
    w = jnp.concatenate([w_router, w_group, jnp.zeros((d, LANES - N_EXPERTS - N_GROUPS), F32)], axis=1).T
    hi = w.astype(BF16)
    lo = (w - hi.astype(F32)).astype(BF16)
    return hi, lo


def _block_sizes(s):
    return dict(tm_proj=512, tq=256, tk=TK, tm_merge=min(512, s), tm_route=512, tm_disp=512, tb=512, tm_comb=512)


def kernel(x, mem, mix_norm, w_in, lam_q1, lam_k1, lam_q2, lam_k2, diff_subln, w_branch_a, w_branch_b, w_gate,
           b_gate, w_out, cross_norm, mem_norm, w_cq, w_ckv, w_co, ffn_norm, w_group, b_group, w_router,
           b_router, w1, w3, w2, final_norm):
    b, s, d = x.shape
    n = b * s
    m = mem.shape[1]
    bs = _block_sizes(s)
    k_sel = min(TOPK_MAX, s // 4)
    row = lambda v: v.reshape(1, -1).astype(F32)

    w_n, w_t = _prep_w_in(w_in[0])
    ka, kb2, kik, qat, qbt, qit, vat, vbt, wit = _proj(x.reshape(n, d), row(mix_norm[0]), w_n, w_t, bs["tm_proj"])
    ka, kb2, kik = [o.reshape(b, s, -1) for o in (ka, kb2, kik)]
    qat, qbt, qit, vat, vbt = [o.reshape(b, s // TK, -1, TK) for o in (qat, qbt, qit, vat, vbt)]
    lam = (jnp.exp(jnp.sum(lam_q1[0].astype(F32) * lam_k1[0].astype(F32)))
           - jnp.exp(jnp.sum(lam_q2[0].astype(F32) * lam_k2[0].astype(F32))) + LAMBDA_INIT).reshape(1)
    subln_t = jnp.broadcast_to(diff_subln[0].astype(F32)[:, None], (LANES, LANES))
    ya = _diff_attention(qat, ka, vat, _sigma_parts(H_A), lam, subln_t, bs["tq"], bs["tk"])
    yb = _dsa_attention(qbt, kb2, vbt, qit, kik, wit, _sigma_parts(H_B), k_sel, bs["tq"], bs["tk"])

    mkv = _memkv(mem.reshape(b * m, d), row(mem_norm[0]), w_ckv[0].astype(BF16), min(512, b * m))
    w_r_hi, w_r_lo = _prep_router(w_group[0], w_router[0])
    params = dict(mix_g=row(mix_norm[0]), w_gate=w_gate[0].astype(BF16), b_gate=row(b_gate[0]),
                  w_a=w_branch_a[0].astype(BF16), w_b=w_branch_b[0].astype(BF16), w_out=w_out[0].astype(BF16),
                  cross_g=row(cross_norm[0]), w_cq=(w_cq[0] * (d // H_X) ** -0.5).astype(BF16),
                  w_co=w_co[0].astype(BF16), ffn_g=row(ffn_norm[0]), w_r_hi=w_r_hi, w_r_lo=w_r_lo)
    h2, lg_t = _merge(x, ya, yb, mkv.reshape(b, m, 2 * d), params, bs["tm_merge"])
    h2 = h2.reshape(n, d)

    bias = jnp.concatenate([b_router[0], b_group[0], jnp.zeros((LANES - N_EXPERTS - N_GROUPS,), F32)])
    mi, mf, cnt = _route(lg_t, bias.reshape(LANES, 1).astype(F32), bs["tm_route"])
    tb = bs["tb"]
    counts = cnt[:, 0].astype(I32)
    padded = (counts + tb - 1) // tb * tb
    pad_end = jnp.cumsum(padded)
    pad_start = (pad_end - padded).astype(I32)
    n_blocks = (2 * n) // tb + N_EXPERTS
    blk_first = jnp.arange(n_blocks, dtype=I32) * tb
    block_expert = jnp.minimum(jnp.sum((pad_end[None, :] <= blk_first[:, None]).astype(I32), axis=1), N_EXPERTS - 1)
    n_used = (pad_end[-1] // tb).astype(I32).reshape(1)
    start_of = jnp.sum(jnp.where(mi[0:2][None] == jnp.arange(N_EXPERTS, dtype=I32)[:, None, None],
                                 pad_start[:, None, None], 0), axis=0)
    dest = start_of + mi[2:4]

    def per_tile(tm):
        return dest.reshape(2, n // tm, tm).transpose(1, 0, 2).reshape(n // tm, 1, 2 * tm)

    fill = jnp.stack([jnp.concatenate([pad_start + counts, n_used]),
                      jnp.concatenate([padded - counts, n_blocks - n_used]),
                      jnp.concatenate([pad_end, n_used])]).astype(I32)
    xs = _dispatch(per_tile(bs["tm_disp"]), fill, h2, n_blocks * tb, bs["tm_disp"], tb)
    y = _experts(block_expert, n_used, xs, row(ffn_norm[0]), w1[0].astype(BF16), w3[0].astype(BF16),
                 w2[0].astype(BF16), tb)
    out = _combine(per_tile(bs["tm_comb"]), mf, h2, row(final_norm), y, bs["tm_comb"])
    return out.reshape(b, s, d)
```
